```python
import math
import jax
import jax.numpy as jnp
from jax import lax
import numpy as np

D_MODEL = 2048
BATCH = 8
SEQ = 2048
DEPTH = 4
DEC_BATCH = 32
DEC_SEQ = 64
PAST_LEN = 2048

CHUNK = 64
N_META = 16
N_BRANCH = 4
BRANCH_W = 512
H_A = 4
DK_A = 128
DV_A = 128
HGRN_BLOCK = 16
H_B = 4
DK_B = 128
DV_B = 128
RET_BLOCK = 64
RET_THETA = 10000.0
H_C = 4
DH_C = 64
DV_C = 128
PARTIAL_ROT = DH_C // 4
H_D = 4
Q_LORA = 512
KV_LORA = 256
D_NOPE = 128
D_ROPE = 64
DV_D = 128
D_FF = 5632
ROPE_THETA = 500000.0
Q_BLOCK = 128
EPS = 1e-5
ALPHA = (2 * DEPTH) ** 0.25
BETA = (8 * DEPTH) ** -0.25
IN_SIZES = (H_A * DK_A, H_A * DK_A, H_A * DV_A, H_A * DV_A,
            H_B * DK_B, H_B * DK_B, H_B * DV_B, H_B * DV_B,
            H_C * 2 * DH_C, H_C * 2 * DH_C, H_C * DV_C,
            Q_LORA, KV_LORA, D_ROPE)
D_IN = sum(IN_SIZES)
SPLIT_IDX = tuple(int(s) for s in np.cumsum(IN_SIZES)[:-1])

kernel_name = 'hybrid_streaming_encoder_step'


def layer_norm(x, g, b):
    xf = x.astype(jnp.float32)
    mu = jnp.mean(xf, -1, keepdims=True)
    var = jnp.mean(jnp.square(xf - mu), -1, keepdims=True)
    return ((xf - mu) * lax.rsqrt(var + EPS) * g.astype(jnp.float32) + b.astype(jnp.float32)).astype(x.dtype)


def rms_norm(x, g):
    xf = x.astype(jnp.float32)
    return (xf * lax.rsqrt(jnp.mean(jnp.square(xf), -1, keepdims=True) + EPS) * g.astype(jnp.float32)).astype(x.dtype)


def head_group_norm(x, g):
    xf = x.astype(jnp.float32)
    mu = jnp.mean(xf, -1, keepdims=True)
    var = jnp.mean(jnp.square(xf - mu), -1, keepdims=True)
    return ((xf - mu) * lax.rsqrt(var + EPS) * g.astype(jnp.float32)).astype(x.dtype)


def rope(x, pos, rot_dim, theta):
    half = rot_dim // 2
    freq = jnp.power(jnp.float32(theta), -jnp.arange(half, dtype=jnp.float32) / half)
    ang = pos.astype(jnp.float32)[:, None] * freq[None, :]
    cos = jnp.cos(ang)[None, :, None, :]
    sin = jnp.sin(ang)[None, :, None, :]
    x1 = x[..., :half].astype(jnp.float32)
    x2 = x[..., half:rot_dim].astype(jnp.float32)
    rot = jnp.concatenate([x1 * cos - x2 * sin, x2 * cos + x1 * sin], -1).astype(x.dtype)
    return jnp.concatenate([rot, x[..., rot_dim:]], -1)


def swiglu(x, w_in, w_out):
    a, b = jnp.split(x @ w_in, 2, axis=-1)
    return (jax.nn.silu(a) * b) @ w_out


def gated_recurrence(q, k, v, log_f, s0, block):
    B, T, H, _ = q.shape
    n = -(-T // block)
    pad = n * block - T

    def blocks(t):
        t = jnp.pad(t.astype(jnp.float32), ((0, 0), (0, pad), (0, 0), (0, 0)))
        return t.reshape(B, n, block, H, t.shape[-1]).swapaxes(0, 1)

    causal = jnp.tril(jnp.ones((block, block), dtype=bool))

    def step(S, xs):
        qb, kb, vb, gb = xs
        b = jnp.cumsum(gb, axis=1)
        o_inter = jnp.einsum('blhk,bhkv->blhv', qb * jnp.exp(b), S)
        diff = b[:, :, None] - b[:, None, :]
        decay = jnp.exp(jnp.where(causal[None, :, :, None, None], diff, -jnp.inf))
        if gb.shape[-1] == 1:
            A = jnp.einsum('bthk,bshk->btsh', qb, kb) * decay[..., 0]
        else:
            A = jnp.sum(qb[:, :, None] * kb[:, None] * decay, axis=-1)
        o_intra = jnp.einsum('btsh,bshv->bthv', A, vb)
        b_last = b[:, -1]
        S = jnp.exp(b_last)[..., None] * S + jnp.einsum('bshk,bshv->bhkv', kb * jnp.exp(b_last[:, None] - b), vb)
        return S, o_inter + o_intra

    S, o = lax.scan(step, s0.astype(jnp.float32), (blocks(q), blocks(k), blocks(v), blocks(log_f)))
    o = o.swapaxes(0, 1).reshape(B, n * block, H, -1)[:, :T]
    return o.astype(v.dtype), S.astype(s0.dtype)


def _attend(q, k, v, q_cid, k_cid, scale):
    s = jnp.einsum('bqhd,bkhd->bhqk', q, k, preferred_element_type=jnp.float32) * scale
    mask = k_cid[None, :] <= q_cid[:, None]
    p = jax.nn.softmax(jnp.where(mask[None, None], s, -jnp.inf), axis=-1)
    return jnp.einsum('bhqk,bkhv->bqhv', p.astype(v.dtype), v)


def chunk_causal_attention(q, k, v, q_cid, k_cid, scale):
    B, Tq, H, dk = q.shape
    if Tq <= Q_BLOCK:
        return _attend(q, k, v, q_cid, k_cid, scale)
    nb = -(-Tq // Q_BLOCK)
    pad = nb * Q_BLOCK - Tq
    qp = jnp.pad(q, ((0, 0), (0, pad), (0, 0), (0, 0))).reshape(B, nb, Q_BLOCK, H, dk).swapaxes(0, 1)
    cp = jnp.pad(q_cid, (0, pad), mode='edge').reshape(nb, Q_BLOCK)
    out = lax.map(lambda xs: _attend(xs[0], k, v, xs[1], k_cid, scale), (qp, cp))
    return out.swapaxes(0, 1).reshape(B, nb * Q_BLOCK, H, -1)[:, :Tq]


def token_mixers(h, pos, q_cid, past_cid, past_k_c, past_v_c, past_ckv, past_kr, s_a, s_b, layer,
                 w_in, lb, hgrn_g, ret_g, lam_q1, lam_k1, lam_q2, lam_k2, diff_g,
                 mla_qg, mla_kvg, w_uq, w_ukv, w_branch, w_gate, w_out):
    B, T, _ = h.shape
    proj = h @ w_in
    (qa, fa, ia, ga, qb, kb, vb, gb, qc, kc, vc, cq, ckv, kr) = jnp.split(proj, SPLIT_IDX, axis=-1)

    def heads(t, n):
        return t.reshape(B, T, n, -1)

    lbh = lb.astype(jnp.float32).reshape(H_A, DK_A)
    z = heads(fa, H_A).astype(jnp.float32)
    log_f = jnp.logaddexp(jnp.log(lbh), jnp.log1p(-lbh) + jax.nn.log_sigmoid(z))
    k_a = -jnp.expm1(log_f)
    o_a, s_a_new = gated_recurrence(jax.nn.silu(heads(qa, H_A)), k_a, heads(ia, H_A), log_f, s_a, HGRN_BLOCK)
    y_a = (rms_norm(o_a, hgrn_g) * jax.nn.silu(heads(ga, H_A))).reshape(B, T, -1)

    q_b = rope(heads(qb, H_B), pos, DK_B, RET_THETA)
    k_b = rope(heads(kb, H_B), pos, DK_B, RET_THETA) * (DK_B ** -0.5)
    log_gamma = jnp.log1p(-jnp.exp2(-5.0 - jnp.arange(H_B, dtype=jnp.float32)))
    log_fb = jnp.broadcast_to(log_gamma[None, None, :, None], (B, T, H_B, 1))
    o_b, s_b_new = gated_recurrence(q_b, k_b, heads(vb, H_B), log_fb, s_b, RET_BLOCK)
    y_b = (head_group_norm(o_b, ret_g) * jax.nn.silu(heads(gb, H_B))).reshape(B, T, -1)

    k_cid = jnp.concatenate([past_cid, q_cid])

    q_c = rope(heads(qc, 2 * H_C), pos, PARTIAL_ROT, ROPE_THETA).reshape(B, T, H_C, 2 * DH_C)
    k_c_new = rope(heads(kc, 2 * H_C), pos, PARTIAL_ROT, ROPE_THETA).reshape(B, T, H_C, 2 * DH_C)
    v_c_new = heads(vc, H_C)
    k_all = jnp.concatenate([past_k_c, k_c_new], axis=1)
    v_all = jnp.concatenate([past_v_c, v_c_new], axis=1)
    a1 = chunk_causal_attention(q_c[..., :DH_C], k_all[..., :DH_C], v_all, q_cid, k_cid, DH_C ** -0.5)
    a2 = chunk_causal_attention(q_c[..., DH_C:], k_all[..., DH_C:], v_all, q_cid, k_cid, DH_C ** -0.5)
    lam_init = 0.8 - 0.6 * math.exp(-0.3 * layer)
    lam = (jnp.exp(jnp.sum(lam_q1.astype(jnp.float32) * lam_k1.astype(jnp.float32)))
           - jnp.exp(jnp.sum(lam_q2.astype(jnp.float32) * lam_k2.astype(jnp.float32))) + lam_init)
    y_c = (rms_norm(a1 - lam.astype(a1.dtype) * a2, diff_g) * (1.0 - lam_init)).reshape(B, T, -1)

    q_d = heads(rms_norm(cq, mla_qg) @ w_uq, H_D)
    q_d = jnp.concatenate([q_d[..., :D_NOPE], rope(q_d[..., D_NOPE:], pos, D_ROPE, ROPE_THETA)], -1)
    ckv_new = rms_norm(ckv, mla_kvg)
    kr_new = rope(kr[:, :, None, :], pos, D_ROPE, ROPE_THETA)[:, :, 0]
    ckv_all = jnp.concatenate([past_ckv, ckv_new], axis=1)
    kr_all = jnp.concatenate([past_kr, kr_new], axis=1)
    Tk = ckv_all.shape[1]
    kv = (ckv_all @ w_ukv).reshape(B, Tk, H_D, D_NOPE + DV_D)
    k_d = jnp.concatenate([kv[..., :D_NOPE], jnp.broadcast_to(kr_all[:, :, None], (B, Tk, H_D, D_ROPE))], -1)
    y_d = chunk_causal_attention(q_d, k_d, kv[..., D_NOPE:], q_cid, k_cid,
                                 (D_NOPE + D_ROPE) ** -0.5).reshape(B, T, -1)

    merged = 0.0
    for n, y in enumerate((y_a, y_b, y_c, y_d)):
        merged = merged + jax.nn.sigmoid(h @ w_gate[n]) * (y @ w_branch[n])
    return merged @ w_out, (k_c_new, v_c_new, ckv_new, kr_new, s_a_new, s_b_new)


def run_trunk(x, pos, q_cid, past_cid, past_k_c, past_v_c, past_ckv, past_kr, s_a, s_b, params):
    new_k, new_v, new_ckv, new_kr, new_sa, new_sb = [], [], [], [], [], []
    for l in range(DEPTH):
        (w_f1i, w_f1o, g1, b1, w_in, lb, hg, rg, lq1, lk1, lq2, lk2, dg, qg, kvg,
         w_uq, w_ukv, w_br, w_gt, w_o, g2, b2, w_f2i, w_f2o, g3, b3) = [p[l] for p in params]
        x = layer_norm(ALPHA * x + 0.5 * swiglu(x, w_f1i, w_f1o), g1, b1)
        m, st = token_mixers(x, pos, q_cid, past_cid, past_k_c[l], past_v_c[l], past_ckv[l], past_kr[l],
                             s_a[l], s_b[l], l, w_in, lb, hg, rg, lq1, lk1, lq2, lk2, dg, qg, kvg,
                             w_uq, w_ukv, w_br, w_gt, w_o)
        x = layer_norm(ALPHA * x + m, g2, b2)
        x = layer_norm(ALPHA * x + 0.5 * swiglu(x, w_f2i, w_f2o), g3, b3)
        new_k.append(st[0]); new_v.append(st[1]); new_ckv.append(st[2])
        new_kr.append(st[3]); new_sa.append(st[4]); new_sb.append(st[5])
    return x, (jnp.stack(new_k), jnp.stack(new_v), jnp.stack(new_ckv), jnp.stack(new_kr),
               jnp.stack(new_sa), jnp.stack(new_sb))


def setup_inputs(seed: int = 0) -> dict:
    key = jax.random.key(seed)
    ks = iter(jax.random.split(key, 48))

    def nrm(shape, scale):
        return jax.random.normal(next(ks), shape, jnp.float32) * scale

    def gain(shape):
        return 1.0 + nrm(shape, 0.02)

    P_TOT = N_META + PAST_LEN
    return {
        'x_prompt': nrm((BATCH, SEQ, D_MODEL), 1.0),
        'x_sample': nrm((DEC_BATCH, DEC_SEQ, D_MODEL), 1.0),
        'cache_diff_k': nrm((DEPTH, DEC_BATCH, P_TOT, H_C, 2 * DH_C), 1.0),
        'cache_diff_v': nrm((DEPTH, DEC_BATCH, P_TOT, H_C, DV_C), 1.0),
        'cache_mla_ckv': nrm((DEPTH, DEC_BATCH, P_TOT, KV_LORA), 1.0),
        'cache_mla_krope': nrm((DEPTH, DEC_BATCH, P_TOT, D_ROPE), 1.0),
        'state_hgrn': nrm((DEPTH, DEC_BATCH, H_A, DK_A, DV_A), 0.5),
        'state_ret': nrm((DEPTH, DEC_BATCH, H_B, DK_B, DV_B), 0.5),
        'meta_tokens': nrm((N_META, D_MODEL), 1.0),
        'w_ffn1_in': nrm((DEPTH, D_MODEL, 2 * D_FF), D_MODEL ** -0.5),
        'w_ffn1_out': nrm((DEPTH, D_FF, D_MODEL), BETA * D_FF ** -0.5),
        'ln1_g': gain((DEPTH, D_MODEL)),
        'ln1_b': nrm((DEPTH, D_MODEL), 0.02),
        'w_in': nrm((DEPTH, D_MODEL, D_IN), D_MODEL ** -0.5),
        'lb_logits': nrm((DEPTH, H_A * DK_A), 1.0),
        'hgrn_norm_g': gain((DEPTH, DV_A)),
        'ret_norm_g': gain((DEPTH, DV_B)),
        'diff_lambda_q1': nrm((DEPTH, DH_C), 0.1),
        'diff_lambda_k1': nrm((DEPTH, DH_C), 0.1),
        'diff_lambda_q2': nrm((DEPTH, DH_C), 0.1),
        'diff_lambda_k2': nrm((DEPTH, DH_C), 0.1),
        'diff_norm_g': gain((DEPTH, DV_C)),
        'mla_q_norm_g': gain((DEPTH, Q_LORA)),
        'mla_kv_norm_g': gain((DEPTH, KV_LORA)),
        'w_mla_uq': nrm((DEPTH, Q_LORA, H_D * (D_NOPE + D_ROPE)), Q_LORA ** -0.5),
        'w_mla_ukv': nrm((DEPTH, KV_LORA, H_D * (D_NOPE + DV_D)), KV_LORA ** -0.5),
        'w_branch': nrm((DEPTH, N_BRANCH, BRANCH_W, D_MODEL), BRANCH_W ** -0.5),
        'w_merge_gate': nrm((DEPTH, N_BRANCH, D_MODEL, D_MODEL), D_MODEL ** -0.5),
        'w_mix_out': nrm((DEPTH, D_MODEL, D_MODEL), BETA * D_MODEL ** -0.5),
        'ln2_g': gain((DEPTH, D_MODEL)),
        'ln2_b': nrm((DEPTH, D_MODEL), 0.02),
        'w_ffn2_in': nrm((DEPTH, D_MODEL, 2 * D_FF), D_MODEL ** -0.5),
        'w_ffn2_out': nrm((DEPTH, D_FF, D_MODEL), BETA * D_FF ** -0.5),
        'ln3_g': gain((DEPTH, D_MODEL)),
        'ln3_b': nrm((DEPTH, D_MODEL), 0.02),
    }


def reference(x_prompt, x_sample, cache_diff_k, cache_diff_v, cache_mla_ckv, cache_mla_krope,
              state_hgrn, state_ret, meta_tokens, w_ffn1_in, w_ffn1_out, ln1_g, ln1_b, w_in, lb_logits,
              hgrn_norm_g, ret_norm_g, diff_lambda_q1, diff_lambda_k1, diff_lambda_q2, diff_lambda_k2,
              diff_norm_g, mla_q_norm_g, mla_kv_norm_g, w_mla_uq, w_mla_ukv, w_branch, w_merge_gate,
              w_mix_out, ln2_g, ln2_b, w_ffn2_in, w_ffn2_out, ln3_g, ln3_b):
    p_soft = jax.nn.softmax(lb_logits.astype(jnp.float32), axis=0)
    lower_bound = jnp.concatenate([jnp.zeros_like(p_soft[:1]), jnp.cumsum(p_soft, axis=0)[:-1]], axis=0)
    params = (w_ffn1_in, w_ffn1_out, ln1_g, ln1_b, w_in, lower_bound, hgrn_norm_g, ret_norm_g,
              diff_lambda_q1, diff_lambda_k1, diff_lambda_q2, diff_lambda_k2, diff_norm_g,
              mla_q_norm_g, mla_kv_norm_g, w_mla_uq, w_mla_ukv, w_branch, w_merge_gate, w_mix_out,
              ln2_g, ln2_b, w_ffn2_in, w_ffn2_out, ln3_g, ln3_b)

    Bp, Sp, _ = x_prompt.shape
    dt = x_prompt.dtype
    xp = jnp.concatenate([jnp.broadcast_to(meta_tokens[None].astype(dt), (Bp, N_META, D_MODEL)), x_prompt], axis=1)
    pos_p = jnp.arange(N_META + Sp, dtype=jnp.int32)
    cid_p = jnp.concatenate([jnp.full((N_META,), -1, jnp.int32), jnp.arange(Sp, dtype=jnp.int32) // CHUNK])
    yp, (pk, pv, pckv, pkr, psa, psb) = run_trunk(
        xp, pos_p, cid_p, jnp.zeros((0,), jnp.int32),
        jnp.zeros((DEPTH, Bp, 0, H_C, 2 * DH_C), dt), jnp.zeros((DEPTH, Bp, 0, H_C, DV_C), dt),
        jnp.zeros((DEPTH, Bp, 0, KV_LORA), dt), jnp.zeros((DEPTH, Bp, 0, D_ROPE), dt),
        jnp.zeros((DEPTH, Bp, H_A, DK_A, DV_A), dt), jnp.zeros((DEPTH, Bp, H_B, DK_B, DV_B), dt), params)
    y_prompt = yp[:, N_META:]

    past = cache_diff_k.shape[2] - N_META
    Ts = x_sample.shape[1]
    past_cid = jnp.concatenate([jnp.full((N_META,), -1, jnp.int32), jnp.arange(past, dtype=jnp.int32) // CHUNK])
    cid_s = (past + jnp.arange(Ts, dtype=jnp.int32)) // CHUNK
    pos_s = N_META + past + jnp.arange(Ts, dtype=jnp.int32)
    y_sample, (sk, sv, sckv, skr, ssa, ssb) = run_trunk(
        x_sample, pos_s, cid_s, past_cid, cache_diff_k, cache_diff_v, cache_mla_ckv, cache_mla_krope,
        state_hgrn, state_ret, params)
    return (y_prompt, y_sample, pk, pv, pckv, pkr, psa, psb, sk, sv, sckv, skr, ssa, ssb)
```

```python
import functools
import math

import numpy as np
import jax
import jax.numpy as jnp
from jax import lax
from jax.experimental import pallas as pl
from jax.experimental.pallas import tpu as pltpu

F32 = jnp.float32
BF16 = jnp.bfloat16

D_MODEL = 2048
CHUNK = 64
N_META = 16
N_BRANCH = 4
BRANCH_W = 512
H_A, DK_A, DV_A = 4, 128, 128
H_B, DK_B, DV_B = 4, 128, 128
RET_THETA = 10000.0
H_C, DH_C, DV_C = 4, 64, 128
PARTIAL_ROT = DH_C // 4
H_D = 4
Q_LORA, KV_LORA = 512, 256
D_NOPE, D_ROPE, DV_D = 128, 64, 128
D_FF = 5632
ROPE_THETA = 500000.0
EPS = 1e-5

LANES = 128
PROJ_TILE = 512
PROJ_COLS = 13 * PROJ_TILE
VMEM_LIMIT = 56 * 1024 * 1024
NEG_BIG = -1e30


def _cparams(sem):
    return pltpu.CompilerParams(dimension_semantics=sem, vmem_limit_bytes=VMEM_LIMIT)


def _pick_tile(n, cap, mult=16):
    best = None
    for t in range(mult, min(n, cap) + 1, mult):
        if n % t == 0:
            best = t
    return best if best is not None else n


def _dot(a, b):
    return jnp.dot(a, b, preferred_element_type=F32)


def _dot_nt(a, b):
    return lax.dot_general(a, b, (((1,), (1,)), ((), ())), preferred_element_type=F32)


def _dot_tn(a, b):
    return lax.dot_general(a, b, (((0,), (0,)), ((), ())), preferred_element_type=F32)


def _layer_norm(z, g, b):
    mu = jnp.mean(z, -1, keepdims=True)
    d = z - mu
    var = jnp.mean(d * d, -1, keepdims=True)
    return d * lax.rsqrt(var + EPS) * g + b


def _rms_norm(z, g):
    return z * lax.rsqrt(jnp.mean(z * z, -1, keepdims=True) + EPS) * g


def _silu(a):
    return a * jax.nn.sigmoid(a)


def _rope128(x, c, s1, s2, shift):
    return x * c + pltpu.roll(x, shift, 1) * s1 + pltpu.roll(x, LANES - shift, 1) * s2


def _lower_bound_kernel(logit_ref, la_ref, lc_ref):
    z = logit_ref[...]
    depth = z.shape[0]
    m = z[0:1]
    for l in range(1, depth):
        m = jnp.maximum(m, z[l:l + 1])
    e = jnp.exp(z - m)
    tot = e[0:1]
    for l in range(1, depth):
        tot = tot + e[l:l + 1]
    p = e / tot
    run = jnp.zeros_like(m)
    for l in range(depth):
        la_ref[l:l + 1, :] = jnp.log(run)
        lc_ref[l:l + 1, :] = jnp.log1p(-run)
        run = run + p[l:l + 1]


def _lower_bounds(lb_logits):
    shp = jax.ShapeDtypeStruct(lb_logits.shape, F32)
    return pl.pallas_call(_lower_bound_kernel, out_shape=(shp, shp), name="lower_bounds")(
        lb_logits.astype(F32))


def _ffn_kernel(x_ref, wa_ref, wb_ref, wo_ref, g_ref, b_ref, y32_ref, y16_ref, xs_ref, acc_ref, *, alpha):
    f = pl.program_id(1)

    @pl.when(f == 0)
    def _():
        xs_ref[...] = x_ref[...].astype(BF16)
        acc_ref[...] = jnp.zeros_like(acc_ref)

    x = xs_ref[...]
    a = _dot(x, wa_ref[...])
    b = _dot(x, wb_ref[...])
    h = (_silu(a) * b).astype(BF16)
    acc_ref[...] += _dot(h, wo_ref[...])

    @pl.when(f == pl.num_programs(1) - 1)
    def _():
        y = _layer_norm(alpha * x_ref[...] + 0.5 * acc_ref[...], g_ref[...], b_ref[...])
        y32_ref[...] = y
        y16_ref[...] = y.astype(BF16)


def _ffn(x32, w_in16, w_out16, g, b, alpha):
    M, D = x32.shape
    F = w_out16.shape[0]
    tm = _pick_tile(M, 640)
    tf = _pick_tile(F, 512, LANES)
    nf = F // tf
    return pl.pallas_call(
        functools.partial(_ffn_kernel, alpha=alpha),
        grid=(M // tm, nf),
        in_specs=[
            pl.BlockSpec((tm, D), lambda i, f: (i, 0)),
            pl.BlockSpec((D, tf), lambda i, f: (0, f)),
            pl.BlockSpec((D, tf), lambda i, f: (0, nf + f)),
            pl.BlockSpec((tf, D), lambda i, f: (f, 0)),
            pl.BlockSpec((1, D), lambda i, f: (0, 0)),
            pl.BlockSpec((1, D), lambda i, f: (0, 0)),
        ],
        out_specs=(pl.BlockSpec((tm, D), lambda i, f: (i, 0)),
                   pl.BlockSpec((tm, D), lambda i, f: (i, 0))),
        out_shape=(jax.ShapeDtypeStruct((M, D), F32), jax.ShapeDtypeStruct((M, D), BF16)),
        scratch_shapes=[pltpu.VMEM((tm, D), BF16), pltpu.VMEM((tm, D), F32)],
        compiler_params=_cparams(("parallel", "arbitrary")),
        name="ffn",
    )(x32, w_in16, w_in16, w_out16, g, b)


def _proj_kernel(x_ref, w_ref, lb_ref, qg_ref, kvg_ref, tab_ref, o_ref):
    j = pl.program_id(1)
    acc = _dot(x_ref[...], w_ref[...])

    def tab(k):
        return tab_ref[:, k * LANES:(k + 1) * LANES]

    def rope_tile(t0, shift, scale):
        c, s1, s2 = tab(t0), tab(t0 + 1), tab(t0 + 2)
        for h in range(PROJ_TILE // LANES):
            r = _rope128(acc[:, h * LANES:(h + 1) * LANES], c, s1, s2, shift)
            if scale is not None:
                r = r * scale
            o_ref[:, h * LANES:(h + 1) * LANES] = r

    @pl.when((j == 0) | (j == 3) | (j == 7))
    def _():
        o_ref[...] = _silu(acc)

    @pl.when(j == 1)
    def _():
        log_sig = -(jnp.maximum(-acc, 0.0) + jnp.log1p(jnp.exp(-jnp.abs(acc))))
        a = lb_ref[0:1, :]
        c = lb_ref[1:2, :] + log_sig
        o_ref[...] = jnp.maximum(a, c) + jnp.log1p(jnp.exp(-jnp.abs(a - c)))

    @pl.when((j == 2) | (j == 6) | (j == 10))
    def _():
        o_ref[...] = acc

    @pl.when(j == 4)
    def _():
        rope_tile(0, DK_B // 2, None)

    @pl.when(j == 5)
    def _():
        rope_tile(0, DK_B // 2, DK_B ** -0.5)

    @pl.when((j == 8) | (j == 9))
    def _():
        rope_tile(3, PARTIAL_ROT // 2, None)

    @pl.when(j == 11)
    def _():
        o_ref[...] = _rms_norm(acc, qg_ref[...])

    @pl.when(j == 12)
    def _():
        o_ref[:, :KV_LORA] = _rms_norm(acc[:, :KV_LORA], kvg_ref[...])
        o_ref[:, KV_LORA:KV_LORA + LANES] = _rope128(
            acc[:, KV_LORA:KV_LORA + LANES], tab(6), tab(7), tab(8), D_ROPE // 2)
        o_ref[:, KV_LORA + LANES:] = jnp.zeros((acc.shape[0], PROJ_TILE - KV_LORA - LANES), F32)


def _proj(x16, w16, lb2, qg, kvg, tab):
    M, D = x16.shape
    tm = _pick_tile(M, 640)
    nj = PROJ_COLS // PROJ_TILE
    return pl.pallas_call(
        _proj_kernel,
        grid=(M // tm, nj),
        in_specs=[
            pl.BlockSpec((tm, D), lambda i, j: (i, 0)),
            pl.BlockSpec((D, PROJ_TILE), lambda i, j: (0, j)),
            pl.BlockSpec((2, PROJ_TILE), lambda i, j: (0, 0)),
            pl.BlockSpec((1, Q_LORA), lambda i, j: (0, 0)),
            pl.BlockSpec((1, KV_LORA), lambda i, j: (0, 0)),
            pl.BlockSpec((tm, 9 * LANES), lambda i, j: (i, 0)),
        ],
        out_specs=pl.BlockSpec((tm, PROJ_TILE), lambda i, j: (i, j)),
        out_shape=jax.ShapeDtypeStruct((M, PROJ_COLS), F32),
        compiler_params=_cparams(("parallel", "arbitrary")),
        name="proj",
    )(x16, w16, lb2, qg, kvg, tab)


def _q_up_kernel(x_ref, w_ref, tab_ref, o_ref):
    acc = _dot(x_ref[...].astype(BF16), w_ref[...])
    nope = H_D * D_NOPE
    o_ref[:, :nope] = acc[:, :nope].astype(BF16)
    c, s1, s2 = (tab_ref[:, k * LANES:(k + 1) * LANES] for k in range(3))
    for h in range(H_D):
        lo = nope + h * LANES
        o_ref[:, lo:lo + LANES] = _rope128(acc[:, lo:lo + LANES], c, s1, s2, D_ROPE // 2).astype(BF16)


def _q_up(P, w16, tab):
    M = P.shape[0]
    N = w16.shape[1]
    tm = _pick_tile(M, 640)
    return pl.pallas_call(
        _q_up_kernel,
        grid=(M // tm,),
        in_specs=[
            pl.BlockSpec((tm, Q_LORA), lambda i: (i, 11)),
            pl.BlockSpec((Q_LORA, N), lambda i: (0, 0)),
            pl.BlockSpec((tm, 3 * LANES), lambda i: (i, 2)),
        ],
        out_specs=pl.BlockSpec((tm, N), lambda i: (i, 0)),
        out_shape=jax.ShapeDtypeStruct((M, N), BF16),
        compiler_params=_cparams(("parallel",)),
        name="q_up",
    )(P, w16, tab)


def _kv_up_kernel(x_ref, w_ref, o_ref):
    o_ref[...] = _dot(x_ref[...].astype(BF16), w_ref[...]).astype(BF16)


def _kv_up(x, col_block, w16):
    M = x.shape[0]
    N = w16.shape[1]
    tm = _pick_tile(M, 1024)
    return pl.pallas_call(
        _kv_up_kernel,
        grid=(M // tm,),
        in_specs=[
            pl.BlockSpec((tm, KV_LORA), lambda i: (i, col_block)),
            pl.BlockSpec((KV_LORA, N), lambda i: (0, 0)),
        ],
        out_specs=pl.BlockSpec((tm, N), lambda i: (i, 0)),
        out_shape=jax.ShapeDtypeStruct((M, N), BF16),
        compiler_params=_cparams(("parallel",)),
        name="kv_up",
    )(x, w16)


def _merge_kernel(h_ref, y_ref, wg_ref, wb_ref, o_ref, acc_ref):
    n = pl.program_id(2)

    @pl.when(n == 0)
    def _():
        acc_ref[...] = jnp.zeros_like(acc_ref)

    gate = jax.nn.sigmoid(_dot(h_ref[...], wg_ref[...]))
    acc_ref[...] += gate * _dot(y_ref[...], wb_ref[...])

    @pl.when(n == pl.num_programs(2) - 1)
    def _():
        o_ref[...] = acc_ref[...].astype(BF16)


def _merge(h16, y_all, wg16, wb16):
    M, D = h16.shape
    nb, _, W = y_all.shape
    tm = _pick_tile(M, 640)
    tn = 512
    return pl.pallas_call(
        _merge_kernel,
        grid=(M // tm, D // tn, nb),
        in_specs=[
            pl.BlockSpec((tm, D), lambda i, j, n: (i, 0)),
            pl.BlockSpec((None, tm, W), lambda i, j, n: (n, i, 0)),
            pl.BlockSpec((None, D, tn), lambda i, j, n: (n, 0, j)),
            pl.BlockSpec((None, W, tn), lambda i, j, n: (n, 0, j)),
        ],
        out_specs=pl.BlockSpec((tm, tn), lambda i, j, n: (i, j)),
        out_shape=jax.ShapeDtypeStruct((M, D), BF16),
        scratch_shapes=[pltpu.VMEM((tm, tn), F32)],
        compiler_params=_cparams(("parallel", "arbitrary", "arbitrary")),
        name="merge",
    )(h16, y_all, wg16, wb16)


def _mix_out_kernel(x_ref, m_ref, w_ref, g_ref, b_ref, y32_ref, y16_ref, *, alpha):
    y = _layer_norm(alpha * x_ref[...] + _dot(m_ref[...], w_ref[...]), g_ref[...], b_ref[...])
    y32_ref[...] = y
    y16_ref[...] = y.astype(BF16)


def _mix_out(x32, merged16, w16, g, b, alpha):
    M, D = x32.shape
    tm = _pick_tile(M, 320)
    return pl.pallas_call(
        functools.partial(_mix_out_kernel, alpha=alpha),
        grid=(M // tm,),
        in_specs=[
            pl.BlockSpec((tm, D), lambda i: (i, 0)),
            pl.BlockSpec((tm, D), lambda i: (i, 0)),
            pl.BlockSpec((D, D), lambda i: (0, 0)),
            pl.BlockSpec((1, D), lambda i: (0, 0)),
            pl.BlockSpec((1, D), lambda i: (0, 0)),
        ],
        out_specs=(pl.BlockSpec((tm, D), lambda i: (i, 0)),
                   pl.BlockSpec((tm, D), lambda i: (i, 0))),
        out_shape=(jax.ShapeDtypeStruct((M, D), F32), jax.ShapeDtypeStruct((M, D), BF16)),
        compiler_params=_cparams(("parallel",)),
        name="mix_out",
    )(x32, merged16, w16, g, b)


def _hgrn_kernel(q_ref, lf_ref, v_ref, gate_ref, s0_ref, g_ref, y_ref, s_out_ref, st_ref, o_ref):
    c = pl.program_id(1)
    C = q_ref.shape[0]

    @pl.when(c == 0)
    def _():
        for h in range(H_A):
            st_ref[h] = s0_ref[h].T

    row = lax.broadcasted_iota(jnp.int32, (C, LANES), 0)
    col = lax.broadcasted_iota(jnp.int32, (C, C), 1)
    tril = (col <= lax.broadcasted_iota(jnp.int32, (C, C), 0)).astype(F32)
    ones = jnp.ones((LANES, LANES), BF16)

    for h in range(H_A):
        sl = slice(h * LANES, (h + 1) * LANES)
        q = q_ref[:, sl]
        g = lf_ref[:, sl]
        v = v_ref[:, sl]
        k = 1.0 - jnp.exp(g)
        b = jnp.dot(tril, g, preferred_element_type=F32, precision=lax.Precision.HIGHEST)
        b_last = b[C - 1:C, :]
        st = st_ref[h]
        o_ref[...] = _dot_nt((q * jnp.exp(b)).astype(BF16), st.astype(BF16))
        for s in range(C):
            r0 = (s // 8) * 8
            e = jnp.exp(jnp.where(row[r0:] >= s, b[r0:] - b[s:s + 1], -jnp.inf))
            p = q[r0:] * e * k[s:s + 1]
            p_hi = p.astype(BF16)
            p_lo = (p - p_hi.astype(F32)).astype(BF16)
            a_col = _dot(p_hi, ones) + _dot(p_lo, ones)
            o_ref[r0:, :] += a_col * v[s:s + 1]
        kd = (k * jnp.exp(b_last - b)).astype(BF16)
        st_ref[h] = st * jnp.exp(b_last) + _dot_tn(v.astype(BF16), kd)
        y = _rms_norm(o_ref[...], g_ref[...]) * gate_ref[:, sl]
        y_ref[:, sl] = y.astype(BF16)

    @pl.when(c == pl.num_programs(1) - 1)
    def _():
        for h in range(H_A):
            s_out_ref[h] = st_ref[h].T


def _hgrn(P, s0, g, off, B, T):
    C = min(T, 64)
    nc = T // C
    base = off // C
    width = H_A * DV_A

    def rows(colblk):
        return pl.BlockSpec((C, width), lambda b, c: (base + b * nc + c, colblk))

    return pl.pallas_call(
        _hgrn_kernel,
        grid=(B, nc),
        in_specs=[rows(0), rows(1), rows(2), rows(3),
                  pl.BlockSpec((None, H_A, DK_A, DV_A), lambda b, c: (b, 0, 0, 0)),
                  pl.BlockSpec((1, DV_A), lambda b, c: (0, 0))],
        out_specs=(pl.BlockSpec((C, width), lambda b, c: (b * nc + c, 0)),
                   pl.BlockSpec((None, H_A, DK_A, DV_A), lambda b, c: (b, 0, 0, 0))),
        out_shape=(jax.ShapeDtypeStruct((B * T, width), BF16),
                   jax.ShapeDtypeStruct((B, H_A, DK_A, DV_A), F32)),
        scratch_shapes=[pltpu.VMEM((H_A, DV_A, DK_A), F32), pltpu.VMEM((C, DV_A), F32)],
        compiler_params=_cparams(("parallel", "arbitrary")),
        name="hgrn",
    )(P, P, P, P, s0, g)


def _ret_kernel(q_ref, k_ref, v_ref, gate_ref, s0_ref, g_ref, y_ref, s_out_ref, s_ref):
    c = pl.program_id(1)
    C = q_ref.shape[0]

    @pl.when(c == 0)
    def _():
        s_ref[...] = s0_ref[...]

    ti = lax.broadcasted_iota(jnp.int32, (C, C), 0)
    si = lax.broadcasted_iota(jnp.int32, (C, C), 1)
    dist = (ti - si).astype(F32)
    t1 = (lax.broadcasted_iota(jnp.int32, (C, 1), 0) + 1).astype(F32)

    for h in range(H_B):
        sl = slice(h * LANES, (h + 1) * LANES)
        log_gamma = math.log1p(-(2.0 ** (-5.0 - h)))
        q = q_ref[:, sl]
        k = k_ref[:, sl]
        v16 = v_ref[:, sl].astype(BF16)
        decay = jnp.exp(jnp.where(ti >= si, dist * log_gamma, -jnp.inf))
        a = _dot_nt(q.astype(BF16), k.astype(BF16)) * decay
        s = s_ref[h]
        o = _dot(a.astype(BF16), v16) + _dot((q * jnp.exp(t1 * log_gamma)).astype(BF16), s.astype(BF16))
        kd = (k * jnp.exp((C - t1) * log_gamma)).astype(BF16)
        s_ref[h] = math.exp(C * log_gamma) * s + _dot_tn(kd, v16)
        mu = jnp.mean(o, -1, keepdims=True)
        d = o - mu
        var = jnp.mean(d * d, -1, keepdims=True)
        y = d * lax.rsqrt(var + EPS) * g_ref[...] * gate_ref[:, sl]
        y_ref[:, sl] = y.astype(BF16)

    @pl.when(c == pl.num_programs(1) - 1)
    def _():
        s_out_ref[...] = s_ref[...]


def _ret(P, s0, g, off, B, T):
    C = min(T, 128)
    nc = T // C
    base = off // C
    width = H_B * DV_B

    def rows(colblk):
        return pl.BlockSpec((C, width), lambda b, c: (base + b * nc + c, colblk))

    return pl.pallas_call(
        _ret_kernel,
        grid=(B, nc),
        in_specs=[rows(4), rows(5), rows(6), rows(7),
                  pl.BlockSpec((None, H_B, DK_B, DV_B), lambda b, c: (b, 0, 0, 0)),
                  pl.BlockSpec((1, DV_B), lambda b, c: (0, 0))],
        out_specs=(pl.BlockSpec((C, width), lambda b, c: (b * nc + c, 0)),
                   pl.BlockSpec((None, H_B, DK_B, DV_B), lambda b, c: (b, 0, 0, 0))),
        out_shape=(jax.ShapeDtypeStruct((B * T, width), BF16),
                   jax.ShapeDtypeStruct((B, H_B, DK_B, DV_B), F32)),
        scratch_shapes=[pltpu.VMEM((H_B, DK_B, DV_B), F32)],
        compiler_params=_cparams(("parallel", "arbitrary")),
        name="retention",
    )(P, P, P, P, s0, g)


def _past_blocks(n, cap=512):
    out, r = [], 0
    while r < n:
        w = min(cap, n - r)
        out.append((r, w))
        r += w
    return out


def _softmax_step(carry, s, v16, mask):
    m, l, acc = carry
    if mask is not None:
        s = jnp.where(mask, s, -jnp.inf)
    m_new = jnp.maximum(m, jnp.max(s, -1, keepdims=True))
    p = jnp.exp(s - m_new)
    alpha = jnp.exp(m - m_new)
    l = alpha * l + jnp.sum(p, -1, keepdims=True)
    acc = alpha * acc + _dot(p.astype(BF16), v16)
    return m_new, l, acc


def _attend(rows, dv, score_fn, value_fn, n_past, bq, qi, masked_diag):
    carry = (jnp.full((rows, 1), NEG_BIG, F32), jnp.zeros((rows, 1), F32), jnp.zeros((rows, dv), F32))
    for r0, n in _past_blocks(n_past):
        carry = _softmax_step(carry, score_fn("past", r0, n), value_fn("past", r0, n), None)

    def body(j, carry):
        r0 = pl.multiple_of(j * bq, bq)
        return _softmax_step(carry, score_fn("self", r0, bq), value_fn("self", r0, bq), None)

    carry = lax.fori_loop(0, qi, body, carry)
    r0 = pl.multiple_of(qi * bq, bq)
    mask = None
    if masked_diag:
        qc = (lax.broadcasted_iota(jnp.int32, (rows, bq), 0) % bq) // CHUNK
        kc = lax.broadcasted_iota(jnp.int32, (rows, bq), 1) // CHUNK
        mask = kc <= qc
    m, l, acc = _softmax_step(carry, score_fn("self", r0, bq), value_fn("self", r0, bq), mask)
    return acc / l


def _diff_kernel(*refs, n_past, lam_init):
    if n_past:
        q_ref, ks_ref, vs_ref, kp_ref, vp_ref, lam_ref, g_ref, y_ref = refs
    else:
        q_ref, ks_ref, vs_ref, lam_ref, g_ref, y_ref = refs
        kp_ref = vp_ref = None
    qi = pl.program_id(1)
    bq = q_ref.shape[0]
    lam_p = lam_ref[...]
    lam = (jnp.exp(jnp.sum(lam_p[0:1] * lam_p[1:2], -1, keepdims=True))
           - jnp.exp(jnp.sum(lam_p[2:3] * lam_p[3:4], -1, keepdims=True)) + lam_init)
    lane = lax.broadcasted_iota(jnp.int32, (bq, LANES), 1)
    scale = DH_C ** -0.5
    for h in range(H_C):
        sl = slice(h * LANES, (h + 1) * LANES)
        qh = q_ref[:, sl]
        q2 = jnp.concatenate([jnp.where(lane < DH_C, qh, 0.0), jnp.where(lane >= DH_C, qh, 0.0)], 0).astype(BF16)

        def score_fn(src, r0, n):
            k = (kp_ref if src == "past" else ks_ref)[pl.ds(r0, n), sl]
            return _dot_nt(q2, k.astype(BF16)) * scale

        def value_fn(src, r0, n):
            return (vp_ref if src == "past" else vs_ref)[pl.ds(r0, n), sl].astype(BF16)

        a = _attend(2 * bq, DV_C, score_fn, value_fn, n_past, bq, qi, bq > CHUNK)
        d = a[:bq] - lam * a[bq:]
        y_ref[:, sl] = (_rms_norm(d, g_ref[...]) * (1.0 - lam_init)).astype(BF16)


def _diff_attn(P, past_k, past_v, lam_p, g, off, B, T, lam_init):
    bq = min(T, 2 * CHUNK)
    nq = T // bq
    width = H_C * DV_C
    in_specs = [
        pl.BlockSpec((bq, width), lambda b, i: (off // bq + b * nq + i, 8)),
        pl.BlockSpec((T, width), lambda b, i: (off // T + b, 9)),
        pl.BlockSpec((T, width), lambda b, i: (off // T + b, 10)),
    ]
    args = [P, P, P]
    n_past = 0
    if past_k is not None:
        if isinstance(past_k, tuple):
            _, poff, n_past = past_k
            in_specs += [pl.BlockSpec((n_past, width), lambda b, i: (poff // n_past + b, 9)),
                         pl.BlockSpec((n_past, width), lambda b, i: (poff // n_past + b, 10))]
            args += [P, P]
        else:
            n_past = past_k.shape[1]
            in_specs += [pl.BlockSpec((None, n_past, width), lambda b, i: (b, 0, 0)),
                         pl.BlockSpec((None, n_past, width), lambda b, i: (b, 0, 0))]
            args += [past_k, past_v]
    in_specs += [pl.BlockSpec((4, DH_C), lambda b, i: (0, 0)), pl.BlockSpec((1, DV_C), lambda b, i: (0, 0))]
    args += [lam_p, g]
    return pl.pallas_call(
        functools.partial(_diff_kernel, n_past=n_past, lam_init=lam_init),
        grid=(B, nq),
        in_specs=in_specs,
        out_specs=pl.BlockSpec((bq, width), lambda b, i: (b * nq + i, 0)),
        out_shape=jax.ShapeDtypeStruct((B * T, width), BF16),
        compiler_params=_cparams(("parallel", "arbitrary")),
        name="diff_attn",
    )(*args)


def _mla_kernel(*refs, n_past):
    if n_past:
        q_ref, kvs_ref, krs_ref, kvp_ref, krp_ref, y_ref = refs
    else:
        q_ref, kvs_ref, krs_ref, y_ref = refs
        kvp_ref = krp_ref = None
    qi = pl.program_id(1)
    bq = q_ref.shape[0]
    scale = (D_NOPE + D_ROPE) ** -0.5
    for h in range(H_D):
        qn = q_ref[:, h * D_NOPE:(h + 1) * D_NOPE]
        qr = q_ref[:, H_D * D_NOPE + h * LANES:H_D * D_NOPE + (h + 1) * LANES]
        ksl = slice(h * (D_NOPE + DV_D), h * (D_NOPE + DV_D) + D_NOPE)
        vsl = slice(h * (D_NOPE + DV_D) + D_NOPE, (h + 1) * (D_NOPE + DV_D))

        def score_fn(src, r0, n):
            kv = kvp_ref if src == "past" else kvs_ref
            kr = krp_ref if src == "past" else krs_ref
            s = _dot_nt(qn, kv[pl.ds(r0, n), ksl]) + _dot_nt(qr, kr[pl.ds(r0, n), :].astype(BF16))
            return s * scale

        def value_fn(src, r0, n):
            return (kvp_ref if src == "past" else kvs_ref)[pl.ds(r0, n), vsl]

        a = _attend(bq, DV_D, score_fn, value_fn, n_past, bq, qi, bq > CHUNK)
        y_ref[:, h * DV_D:(h + 1) * DV_D] = a.astype(BF16)


def _mla_attn(q16, kv16, P, past_kv, past_kr, off, B, T):
    bq = min(T, 2 * CHUNK)
    nq = T // bq
    wq = q16.shape[1]
    wkv = kv16.shape[1]
    kr_blk = (KV_LORA * 25) // LANES
    in_specs = [
        pl.BlockSpec((bq, wq), lambda b, i: (off // bq + b * nq + i, 0)),
        pl.BlockSpec((T, wkv), lambda b, i: (off // T + b, 0)),
        pl.BlockSpec((T, LANES), lambda b, i: (off // T + b, kr_blk)),
    ]
    args = [q16, kv16, P]
    n_past = 0
    if past_kv is not None:
        if isinstance(past_kv, tuple):
            _, poff, n_past = past_kv
            in_specs += [pl.BlockSpec((n_past, wkv), lambda b, i: (poff // n_past + b, 0)),
                         pl.BlockSpec((n_past, LANES), lambda b, i: (poff // n_past + b, kr_blk))]
            args += [kv16, P]
        else:
            n_past = past_kv.shape[1]
            in_specs += [pl.BlockSpec((None, n_past, wkv), lambda b, i: (b, 0, 0)),
                         pl.BlockSpec((None, n_past, LANES), lambda b, i: (b, 0, 0))]
            args += [past_kv, past_kr]
    width = H_D * DV_D
    return pl.pallas_call(
        functools.partial(_mla_kernel, n_past=n_past),
        grid=(B, nq),
        in_specs=in_specs,
        out_specs=pl.BlockSpec((bq, width), lambda b, i: (b * nq + i, 0)),
        out_shape=jax.ShapeDtypeStruct((B * T, width), BF16),
        compiler_params=_cparams(("parallel", "arbitrary")),
        name="mla_attn",
    )(*args)


def _rope_table(pos, period, half, rot_dim, theta):
    lane = np.arange(LANES)
    li = lane % period
    first = li < half
    second = (li >= half) & (li < rot_dim)
    idx = np.where(first, li, np.where(second, li - half, 0))
    freq = jnp.power(jnp.float32(theta), -jnp.arange(half, dtype=F32) / half)[idx]
    ang = pos.astype(F32)[:, None] * freq[None, :]
    cos, sin = jnp.cos(ang), jnp.sin(ang)
    rot = jnp.asarray(first | second)[None, :]
    return [jnp.where(rot, cos, 1.0), jnp.where(jnp.asarray(second)[None, :], sin, 0.0),
            jnp.where(jnp.asarray(first)[None, :], -sin, 0.0)]


def _rope_tables(pos):
    tabs = (_rope_table(pos, LANES, DK_B // 2, DK_B, RET_THETA)
            + _rope_table(pos, DH_C, PARTIAL_ROT // 2, PARTIAL_ROT, ROPE_THETA)
            + _rope_table(pos, LANES, D_ROPE // 2, D_ROPE, ROPE_THETA))
    return jnp.concatenate(tabs, axis=1)


def _uq_layout(w_uq):
    w = w_uq.reshape(Q_LORA, H_D, D_NOPE + D_ROPE)
    nope = w[:, :, :D_NOPE].reshape(Q_LORA, H_D * D_NOPE)
    rope = jnp.pad(w[:, :, D_NOPE:], ((0, 0), (0, 0), (0, LANES - D_ROPE))).reshape(Q_LORA, H_D * LANES)
    return jnp.concatenate([nope, rope], axis=1).astype(BF16)


def kernel(x_prompt, x_sample, cache_diff_k, cache_diff_v, cache_mla_ckv, cache_mla_krope, state_hgrn, state_ret, meta_tokens, w_ffn1_in, w_ffn1_out, ln1_g, ln1_b, w_in, lb_logits, hgrn_norm_g, ret_norm_g, diff_lambda_q1, diff_lambda_k1, diff_lambda_q2, diff_lambda_k2, diff_norm_g, mla_q_norm_g, mla_kv_norm_g, w_mla_uq, w_mla_ukv, w_branch, w_merge_gate, w_mix_out, ln2_g, ln2_b, w_ffn2_in, w_ffn2_out, ln3_g, ln3_b):
    depth = w_in.shape[0]
    Bp, S, D = x_prompt.shape
    Bs, Ts, _ = x_sample.shape
    n_cache = cache_diff_k.shape[2]
    past = n_cache - N_META
    assert S % (2 * CHUNK) == 0 and Ts <= CHUNK and past % CHUNK == 0 and (past + Ts - 1) // CHUNK == past // CHUNK
    alpha = (2 * depth) ** 0.25

    off_s = Bp * S
    off_m = off_s + Bs * Ts
    M = off_m + Bp * N_META
    assert off_s % Ts == 0 and off_m % N_META == 0
    x = jnp.concatenate([
        x_prompt.reshape(Bp * S, D), x_sample.reshape(Bs * Ts, D),
        jnp.broadcast_to(meta_tokens[None].astype(x_prompt.dtype), (Bp, N_META, D)).reshape(Bp * N_META, D)], 0)
    pos = jnp.concatenate([
        jnp.tile(N_META + jnp.arange(S, dtype=jnp.int32), Bp),
        jnp.tile(N_META + past + jnp.arange(Ts, dtype=jnp.int32), Bs),
        jnp.tile(jnp.arange(N_META, dtype=jnp.int32), Bp)])
    tab = _rope_tables(pos)

    la, lc = _lower_bounds(lb_logits)
    groups = (("meta", off_m, Bp, N_META), ("prompt", 0, Bp, S), ("sample", off_s, Bs, Ts))
    row = lambda a: a.reshape(1, -1).astype(F32)

    outs = {k: [] for k in ("pk", "pv", "pckv", "pkr", "psa", "psb", "sk", "sv", "sckv", "skr", "ssa", "ssb")}
    x16 = None
    for l in range(depth):
        x, x16 = _ffn(x, w_ffn1_in[l].astype(BF16), w_ffn1_out[l].astype(BF16), row(ln1_g[l]), row(ln1_b[l]), alpha)

        w_in16 = jnp.pad(w_in[l], ((0, 0), (0, PROJ_COLS - w_in.shape[2]))).astype(BF16)
        lb2 = jnp.stack([la[l], lc[l]], 0)
        P = _proj(x16, w_in16, lb2, row(mla_q_norm_g[l]), row(mla_kv_norm_g[l]), tab)
        q_d = _q_up(P, _uq_layout(w_mla_uq[l]), tab)
        w_ukv16 = w_mla_ukv[l].astype(BF16)
        kv_new = _kv_up(P, 24, w_ukv16)
        kv_past = _kv_up(cache_mla_ckv[l].reshape(Bs * n_cache, KV_LORA), 0, w_ukv16).reshape(Bs, n_cache, -1)
        kr_past = jnp.pad(cache_mla_krope[l], ((0, 0), (0, 0), (0, LANES - D_ROPE)))
        ck_past = cache_diff_k[l].reshape(Bs, n_cache, H_C * 2 * DH_C)
        cv_past = cache_diff_v[l].reshape(Bs, n_cache, H_C * DV_C)
        lam_p = jnp.stack([diff_lambda_q1[l], diff_lambda_k1[l], diff_lambda_q2[l], diff_lambda_k2[l]], 0).astype(F32)
        lam_init = 0.8 - 0.6 * math.exp(-0.3 * l)

        ys = {}
        sa_meta = sb_meta = None
        for name, off, B, T in groups:
            if name == "meta":
                sa0 = jnp.zeros((B, H_A, DK_A, DV_A), F32)
                sb0 = jnp.zeros((B, H_B, DK_B, DV_B), F32)
                past_c = past_d = (None, None)
            elif name == "prompt":
                sa0, sb0 = sa_meta, sb_meta
                past_c = past_d = (("flat", off_m, N_META),) * 2
            else:
                sa0, sb0 = state_hgrn[l].astype(F32), state_ret[l].astype(F32)
                past_c, past_d = (ck_past, cv_past), (kv_past, kr_past)
            ya, sa = _hgrn(P, sa0, row(hgrn_norm_g[l]), off, B, T)
            yb, sb = _ret(P, sb0, row(ret_norm_g[l]), off, B, T)
            yc = _diff_attn(P, past_c[0], past_c[1], lam_p, row(diff_norm_g[l]), off, B, T, lam_init)
            yd = _mla_attn(q_d, kv_new, P, past_d[0], past_d[1], off, B, T)
            ys[name] = (ya, yb, yc, yd)
            if name == "meta":
                sa_meta, sb_meta = sa, sb
            elif name == "prompt":
                outs["psa"].append(sa); outs["psb"].append(sb)
            else:
                outs["ssa"].append(sa); outs["ssb"].append(sb)

        y_all = jnp.stack([jnp.concatenate([ys["prompt"][n], ys["sample"][n], ys["meta"][n]], 0)
                           for n in range(N_BRANCH)], 0)
        merged = _merge(x16, y_all, w_merge_gate[l].astype(BF16), w_branch[l].astype(BF16))
        x, x16 = _mix_out(x, merged, w_mix_out[l].astype(BF16), row(ln2_g[l]), row(ln2_b[l]), alpha)
        x, x16 = _ffn(x, w_ffn2_in[l].astype(BF16), w_ffn2_out[l].astype(BF16), row(ln3_g[l]), row(ln3_b[l]), alpha)

        def cache_rows(c0, c1, shape_tail):
            piece = P[:, c0:c1]
            prm = jnp.concatenate([piece[off_m:].reshape(Bp, N_META, -1), piece[:off_s].reshape(Bp, S, -1)], 1)
            smp = piece[off_s:off_m].reshape(Bs, Ts, -1)
            return prm.reshape((Bp, N_META + S) + shape_tail), smp.reshape((Bs, Ts) + shape_tail)

        for key, (c0, c1, tail) in (("k", (9 * PROJ_TILE, 10 * PROJ_TILE, (H_C, 2 * DH_C))),
                                     ("v", (10 * PROJ_TILE, 11 * PROJ_TILE, (H_C, DV_C))),
                                     ("ckv", (12 * PROJ_TILE, 12 * PROJ_TILE + KV_LORA, (KV_LORA,))),
                                     ("kr", (12 * PROJ_TILE + KV_LORA, 12 * PROJ_TILE + KV_LORA + D_ROPE, (D_ROPE,)))):
            prm, smp = cache_rows(c0, c1, tail)
            outs["p" + key].append(prm)
            outs["s" + key].append(smp)

    y_prompt = x[:off_s].reshape(Bp, S, D)
    y_sample = x[off_s:off_m].reshape(Bs, Ts, D)
    st = lambda k: jnp.stack(outs[k], 0)
    return (y_prompt, y_sample, st("pk"), st("pv"), st("pckv"), st("pkr"), st("psa"), st("psb"),
            st("sk"), st("sv"), st("sckv"), st("skr"), st("ssa"), st("ssb"))
```

```python
import functools
import math

import numpy as np
import jax
import jax.numpy as jnp
from jax import lax
from jax.experimental import pallas as pl
from jax.experimental.pallas import tpu as pltpu

F32 = jnp.float32
BF16 = jnp.bfloat16

D_MODEL = 2048
CHUNK = 64
N_META = 16
N_BRANCH = 4
BRANCH_W = 512
H_A, DK_A, DV_A = 4, 128, 128
H_B, DK_B, DV_B = 4, 128, 128
RET_THETA = 10000.0
H_C, DH_C, DV_C = 4, 64, 128
PARTIAL_ROT = DH_C // 4
H_D = 4
Q_LORA, KV_LORA = 512, 256
D_NOPE, D_ROPE, DV_D = 128, 64, 128
D_FF = 5632
ROPE_THETA = 500000.0
EPS = 1e-5

LANES = 128
SUBLANES = 8
PROJ_TILE = 512
PROJ_COLS = 13 * PROJ_TILE
P16_FIRST = 8
P16_COLS = PROJ_COLS - P16_FIRST * PROJ_TILE
VMEM_LIMIT = 56 * 1024 * 1024
NEG_BIG = -1e30
LOG2E = math.log2(math.e)


def _cparams(sem):
    return pltpu.CompilerParams(dimension_semantics=sem, vmem_limit_bytes=VMEM_LIMIT)


def _pick_tile(n, cap, mult=16):
    best = None
    for t in range(mult, min(n, cap) + 1, mult):
        if n % t == 0:
            best = t
    return best if best is not None else n


def _dot(a, b):
    return jnp.dot(a, b, preferred_element_type=F32)


def _dot_nt(a, b):
    return lax.dot_general(a, b, (((1,), (1,)), ((), ())), preferred_element_type=F32)


def _dot_tn(a, b):
    return lax.dot_general(a, b, (((0,), (0,)), ((), ())), preferred_element_type=F32)


def _layer_norm(z, g, b):
    mu = jnp.mean(z, -1, keepdims=True)
    d = z - mu
    var = jnp.mean(d * d, -1, keepdims=True)
    return d * lax.rsqrt(var + EPS) * g + b


def _rms_norm(z, g):
    return z * lax.rsqrt(jnp.mean(z * z, -1, keepdims=True) + EPS) * g


def _silu(a):
    return a * jax.nn.sigmoid(a)


def _rope128(x, c, s1, s2, shift):
    return x * c + pltpu.roll(x, shift, 1) * s1 + pltpu.roll(x, LANES - shift, 1) * s2


def _lower_bound_kernel(logit_ref, la_ref, lc_ref):
    z = logit_ref[...]
    depth = z.shape[0]
    m = z[0:1]
    for l in range(1, depth):
        m = jnp.maximum(m, z[l:l + 1])
    e = jnp.exp(z - m)
    tot = e[0:1]
    for l in range(1, depth):
        tot = tot + e[l:l + 1]
    p = e / tot
    run = jnp.zeros_like(m)
    for l in range(depth):
        la_ref[l:l + 1, :] = jnp.log(run)
        lc_ref[l:l + 1, :] = jnp.log1p(-run)
        run = run + p[l:l + 1]


def _lower_bounds(lb_logits):
    shp = jax.ShapeDtypeStruct(lb_logits.shape, F32)
    return pl.pallas_call(_lower_bound_kernel, out_shape=(shp, shp), name="lower_bounds")(
        lb_logits.astype(F32))


def _ffn_kernel(x_ref, wa_ref, wb_ref, wo_ref, g_ref, b_ref, y32_ref, y16_ref, xs_ref, acc_ref, *, alpha):
    f = pl.program_id(1)

    @pl.when(f == 0)
    def _():
        xs_ref[...] = x_ref[...].astype(BF16)
        acc_ref[...] = jnp.zeros_like(acc_ref)

    x = xs_ref[...]
    a = _dot(x, wa_ref[...])
    b = _dot(x, wb_ref[...])
    h = (_silu(a) * b).astype(BF16)
    acc_ref[...] += _dot(h, wo_ref[...])

    @pl.when(f == pl.num_programs(1) - 1)
    def _():
        y = _layer_norm(alpha * x_ref[...] + 0.5 * acc_ref[...], g_ref[...], b_ref[...])
        y32_ref[...] = y
        y16_ref[...] = y.astype(BF16)


def _ffn(x32, w_in16, w_out16, l, g, b, alpha):
    M, D = x32.shape
    F = w_out16.shape[1]
    tm = _pick_tile(M, 640)
    tf = _pick_tile(F, 512, LANES)
    nf = F // tf
    return pl.pallas_call(
        functools.partial(_ffn_kernel, alpha=alpha),
        grid=(M // tm, nf),
        in_specs=[
            pl.BlockSpec((tm, D), lambda i, f: (i, 0)),
            pl.BlockSpec((None, D, tf), lambda i, f: (l, 0, f)),
            pl.BlockSpec((None, D, tf), lambda i, f: (l, 0, nf + f)),
            pl.BlockSpec((None, tf, D), lambda i, f: (l, f, 0)),
            pl.BlockSpec((1, D), lambda i, f: (0, 0)),
            pl.BlockSpec((1, D), lambda i, f: (0, 0)),
        ],
        out_specs=(pl.BlockSpec((tm, D), lambda i, f: (i, 0)),
                   pl.BlockSpec((tm, D), lambda i, f: (i, 0))),
        out_shape=(jax.ShapeDtypeStruct((M, D), F32), jax.ShapeDtypeStruct((M, D), BF16)),
        scratch_shapes=[pltpu.VMEM((tm, D), BF16), pltpu.VMEM((tm, D), F32)],
        compiler_params=_cparams(("parallel", "arbitrary")),
        name="ffn",
    )(x32, w_in16, w_in16, w_out16, g, b)


def _proj_kernel(x_ref, w_ref, lb_ref, qg_ref, kvg_ref, tab_ref, o_ref, o16_ref):
    j = pl.program_id(1)
    acc = _dot(x_ref[...], w_ref[...])

    def tab(k):
        return tab_ref[:, k * LANES:(k + 1) * LANES]

    def rope_tile(t0, shift, scale, with16):
        c, s1, s2 = tab(t0), tab(t0 + 1), tab(t0 + 2)
        for h in range(PROJ_TILE // LANES):
            r = _rope128(acc[:, h * LANES:(h + 1) * LANES], c, s1, s2, shift)
            if scale is not None:
                r = r * scale
            o_ref[:, h * LANES:(h + 1) * LANES] = r
            if with16:
                o16_ref[:, h * LANES:(h + 1) * LANES] = r.astype(BF16)

    @pl.when((j == 0) | (j == 3) | (j == 7))
    def _():
        o_ref[...] = _silu(acc)

    @pl.when(j == 1)
    def _():
        log_sig = -(jnp.maximum(-acc, 0.0) + jnp.log1p(jnp.exp(-jnp.abs(acc))))
        a = lb_ref[0:1, :]
        c = lb_ref[1:2, :] + log_sig
        o_ref[...] = jnp.maximum(a, c) + jnp.log1p(jnp.exp(-jnp.abs(a - c)))

    @pl.when((j == 2) | (j == 6))
    def _():
        o_ref[...] = acc

    @pl.when(j == 10)
    def _():
        o_ref[...] = acc
        o16_ref[...] = acc.astype(BF16)

    @pl.when(j == 4)
    def _():
        rope_tile(0, DK_B // 2, None, False)

    @pl.when(j == 5)
    def _():
        rope_tile(0, DK_B // 2, DK_B ** -0.5, False)

    @pl.when((j == 8) | (j == 9))
    def _():
        rope_tile(3, PARTIAL_ROT // 2, None, True)

    @pl.when(j == 11)
    def _():
        r = _rms_norm(acc, qg_ref[...])
        o_ref[...] = r
        o16_ref[...] = r.astype(BF16)

    @pl.when(j == 12)
    def _():
        ckv = _rms_norm(acc[:, :KV_LORA], kvg_ref[...])
        kr = _rope128(acc[:, KV_LORA:KV_LORA + LANES], tab(6), tab(7), tab(8), D_ROPE // 2)
        zeros = jnp.zeros((acc.shape[0], PROJ_TILE - KV_LORA - LANES), F32)
        o_ref[:, :KV_LORA] = ckv
        o_ref[:, KV_LORA:KV_LORA + LANES] = kr
        o_ref[:, KV_LORA + LANES:] = zeros
        o16_ref[:, :KV_LORA] = ckv.astype(BF16)
        o16_ref[:, KV_LORA:KV_LORA + LANES] = kr.astype(BF16)
        o16_ref[:, KV_LORA + LANES:] = zeros.astype(BF16)


def _proj(x16, w16, l, lb2, qg, kvg, tab):
    M, D = x16.shape
    tm = _pick_tile(M, 640)
    nj = PROJ_COLS // PROJ_TILE
    return pl.pallas_call(
        _proj_kernel,
        grid=(M // tm, nj),
        in_specs=[
            pl.BlockSpec((tm, D), lambda i, j: (i, 0)),
            pl.BlockSpec((None, D, PROJ_TILE), lambda i, j: (l, 0, j)),
            pl.BlockSpec((2, PROJ_TILE), lambda i, j: (0, 0)),
            pl.BlockSpec((1, Q_LORA), lambda i, j: (0, 0)),
            pl.BlockSpec((1, KV_LORA), lambda i, j: (0, 0)),
            pl.BlockSpec((tm, 9 * LANES), lambda i, j: (i, 0)),
        ],
        out_specs=(pl.BlockSpec((tm, PROJ_TILE), lambda i, j: (i, j)),
                   pl.BlockSpec((tm, PROJ_TILE), lambda i, j: (i, jnp.maximum(j - P16_FIRST, 0)))),
        out_shape=(jax.ShapeDtypeStruct((M, PROJ_COLS), F32), jax.ShapeDtypeStruct((M, P16_COLS), BF16)),
        compiler_params=_cparams(("parallel", "arbitrary")),
        name="proj",
    )(x16, w16, lb2, qg, kvg, tab)


def _q_up_kernel(x_ref, w_ref, tab_ref, o_ref):
    acc = _dot(x_ref[...], w_ref[...])
    nope = H_D * D_NOPE
    o_ref[:, :nope] = acc[:, :nope].astype(BF16)
    c, s1, s2 = (tab_ref[:, k * LANES:(k + 1) * LANES] for k in range(3))
    for h in range(H_D):
        lo = nope + h * LANES
        o_ref[:, lo:lo + LANES] = _rope128(acc[:, lo:lo + LANES], c, s1, s2, D_ROPE // 2).astype(BF16)


def _q_up(P16, w16, l, tab):
    M = P16.shape[0]
    N = w16.shape[2]
    tm = _pick_tile(M, 640)
    return pl.pallas_call(
        _q_up_kernel,
        grid=(M // tm,),
        in_specs=[
            pl.BlockSpec((tm, Q_LORA), lambda i: (i, 11 - P16_FIRST)),
            pl.BlockSpec((None, Q_LORA, N), lambda i: (l, 0, 0)),
            pl.BlockSpec((tm, 3 * LANES), lambda i: (i, 2)),
        ],
        out_specs=pl.BlockSpec((tm, N), lambda i: (i, 0)),
        out_shape=jax.ShapeDtypeStruct((M, N), BF16),
        compiler_params=_cparams(("parallel",)),
        name="q_up",
    )(P16, w16, tab)


def _kv_up_kernel(x_ref, w_ref, o_ref):
    o_ref[...] = _dot(x_ref[...].astype(BF16), w_ref[...]).astype(BF16)


def _kv_up(x, col_block, w16, l):
    M = x.shape[0]
    N = w16.shape[2]
    tm = _pick_tile(M, 1024)
    return pl.pallas_call(
        _kv_up_kernel,
        grid=(M // tm,),
        in_specs=[
            pl.BlockSpec((tm, KV_LORA), lambda i: (i, col_block)),
            pl.BlockSpec((None, KV_LORA, N), lambda i: (l, 0, 0)),
        ],
        out_specs=pl.BlockSpec((tm, N), lambda i: (i, 0)),
        out_shape=jax.ShapeDtypeStruct((M, N), BF16),
        compiler_params=_cparams(("parallel",)),
        name="kv_up",
    )(x, w16)


def _merge_kernel(h_ref, ya_ref, yb_ref, yc_ref, yd_ref, wg_ref, wb_ref, o_ref, acc_ref):
    n = pl.program_id(2)

    @pl.when(n == 0)
    def _():
        acc_ref[...] = jnp.zeros_like(acc_ref)

    gate = jax.nn.sigmoid(_dot(h_ref[...], wg_ref[...]))
    for idx, y_ref in enumerate((ya_ref, yb_ref, yc_ref, yd_ref)):
        @pl.when(n == idx)
        def _(y_ref=y_ref):
            acc_ref[...] += gate * _dot(y_ref[...], wb_ref[...])

    @pl.when(n == pl.num_programs(2) - 1)
    def _():
        o_ref[...] = acc_ref[...].astype(BF16)


def _merge(h16, ys, wg16, wb16, l):
    M, D = h16.shape
    W = ys[0].shape[1]
    tm = _pick_tile(M, 640)
    tn = 512
    y_spec = pl.BlockSpec((tm, W), lambda i, j, n: (i, 0))
    return pl.pallas_call(
        _merge_kernel,
        grid=(M // tm, D // tn, N_BRANCH),
        in_specs=[
            pl.BlockSpec((tm, D), lambda i, j, n: (i, 0)),
            y_spec, y_spec, y_spec, y_spec,
            pl.BlockSpec((None, None, D, tn), lambda i, j, n: (l, n, 0, j)),
            pl.BlockSpec((None, None, W, tn), lambda i, j, n: (l, n, 0, j)),
        ],
        out_specs=pl.BlockSpec((tm, tn), lambda i, j, n: (i, j)),
        out_shape=jax.ShapeDtypeStruct((M, D), BF16),
        scratch_shapes=[pltpu.VMEM((tm, tn), F32)],
        compiler_params=_cparams(("parallel", "arbitrary", "arbitrary")),
        name="merge",
    )(h16, *ys, wg16, wb16)


def _mix_out_kernel(x_ref, m_ref, w_ref, g_ref, b_ref, y32_ref, y16_ref, *, alpha):
    y = _layer_norm(alpha * x_ref[...] + _dot(m_ref[...], w_ref[...]), g_ref[...], b_ref[...])
    y32_ref[...] = y
    y16_ref[...] = y.astype(BF16)


def _mix_out(x32, merged16, w16, l, g, b, alpha):
    M, D = x32.shape
    tm = _pick_tile(M, 320)
    return pl.pallas_call(
        functools.partial(_mix_out_kernel, alpha=alpha),
        grid=(M // tm,),
        in_specs=[
            pl.BlockSpec((tm, D), lambda i: (i, 0)),
            pl.BlockSpec((tm, D), lambda i: (i, 0)),
            pl.BlockSpec((None, D, D), lambda i: (l, 0, 0)),
            pl.BlockSpec((1, D), lambda i: (0, 0)),
            pl.BlockSpec((1, D), lambda i: (0, 0)),
        ],
        out_specs=(pl.BlockSpec((tm, D), lambda i: (i, 0)),
                   pl.BlockSpec((tm, D), lambda i: (i, 0))),
        out_shape=(jax.ShapeDtypeStruct((M, D), F32), jax.ShapeDtypeStruct((M, D), BF16)),
        compiler_params=_cparams(("parallel",)),
        name="mix_out",
    )(x32, merged16, w16, g, b)


def _y_alias(y_prev, n_inputs):
    if y_prev is None:
        return [], [], {}
    return [pl.BlockSpec(memory_space=pl.ANY)], [y_prev], {n_inputs: 0}


def _hgrn_kernel(q_ref, lf_ref, v_ref, gate_ref, s0_ref, g_ref, *rest, C):
    y_ref, s_out_ref, st_ref = rest[-3:]
    step = pl.program_id(1)
    n_sub = q_ref.shape[0] // C
    nv = C // SUBLANES

    @pl.when(step == 0)
    def _():
        for h in range(H_A):
            st_ref[h] = s0_ref[h].T

    row8 = lax.broadcasted_iota(jnp.int32, (SUBLANES, LANES), 0)
    lane8 = lax.broadcasted_iota(jnp.int32, (SUBLANES, C), 1)
    tril = (lax.broadcasted_iota(jnp.int32, (C, C), 1) <= lax.broadcasted_iota(jnp.int32, (C, C), 0)).astype(F32)

    def chunk(ci, carry):
        rs = pl.ds(pl.multiple_of(ci * C, C), C)
        for h in range(H_A):
            sl = slice(h * LANES, (h + 1) * LANES)
            q = q_ref[rs, sl]
            g = lf_ref[rs, sl]
            v16 = v_ref[rs, sl].astype(BF16)
            k = 1.0 - jnp.exp(g)
            b = jnp.dot(tril, g, preferred_element_type=F32, precision=lax.Precision.HIGHEST)
            b2 = b * LOG2E
            b_last = b[C - 1:C, :]
            st = st_ref[h]
            o = _dot_nt((q * jnp.exp(b)).astype(BF16), st.astype(BF16))
            pieces = []
            for s in range(C):
                r0 = (s // SUBLANES) * SUBLANES
                d = b2[r0:] - b2[s:s + 1]
                head = jnp.where(row8 >= s % SUBLANES, d[:SUBLANES], -jnp.inf)
                d = head if C - r0 == SUBLANES else jnp.concatenate([head, d[SUBLANES:]], 0)
                pieces.append(q[r0:] * jnp.exp2(d))
            res = _dot_nt(jnp.concatenate(pieces, 0).astype(BF16), k.astype(BF16))
            a_parts = [jnp.zeros((SUBLANES, C), F32) for _ in range(nv)]
            off = 0
            for s in range(C):
                v0 = s // SUBLANES
                for i in range(v0, nv):
                    a_parts[i] = a_parts[i] + jnp.where(lane8 == s, res[off:off + SUBLANES], 0.0)
                    off += SUBLANES
            a = jnp.concatenate(a_parts, 0)
            o = o + _dot(a.astype(BF16), v16)
            kd = (k * jnp.exp(b_last - b)).astype(BF16)
            st_ref[h] = st * jnp.exp(b_last) + _dot_tn(v16, kd)
            y_ref[rs, sl] = (_rms_norm(o, g_ref[...]) * gate_ref[rs, sl]).astype(BF16)
        return carry

    lax.fori_loop(0, n_sub, chunk, 0)

    @pl.when(step == pl.num_programs(1) - 1)
    def _():
        for h in range(H_A):
            s_out_ref[h] = st_ref[h].T


def _hgrn(P, s0, g, off, B, T, M, y_prev):
    C = min(T, 64)
    rows_blk = min(T, 256)
    nb = T // rows_blk
    base = off // rows_blk
    width = H_A * DV_A

    def rows(colblk):
        return pl.BlockSpec((rows_blk, width), lambda b, c: (base + b * nb + c, colblk))

    a_specs, a_args, aliases = _y_alias(y_prev, 6)
    return pl.pallas_call(
        functools.partial(_hgrn_kernel, C=C),
        grid=(B, nb),
        in_specs=[rows(0), rows(1), rows(2), rows(3),
                  pl.BlockSpec((None, H_A, DK_A, DV_A), lambda b, c: (b, 0, 0, 0)),
                  pl.BlockSpec((1, DV_A), lambda b, c: (0, 0))] + a_specs,
        out_specs=(pl.BlockSpec((rows_blk, width), lambda b, c: (base + b * nb + c, 0)),
                   pl.BlockSpec((None, H_A, DK_A, DV_A), lambda b, c: (b, 0, 0, 0))),
        out_shape=(jax.ShapeDtypeStruct((M, width), BF16),
                   jax.ShapeDtypeStruct((B, H_A, DK_A, DV_A), F32)),
        scratch_shapes=[pltpu.VMEM((H_A, DV_A, DK_A), F32)],
        input_output_aliases=aliases,
        compiler_params=_cparams(("parallel", "arbitrary")),
        name="hgrn",
    )(P, P, P, P, s0, g, *a_args)


def _ret_kernel(q_ref, k_ref, v_ref, gate_ref, s0_ref, g_ref, *rest):
    y_ref, s_out_ref, s_ref = rest[-3:]
    c = pl.program_id(1)
    C = q_ref.shape[0]

    @pl.when(c == 0)
    def _():
        s_ref[...] = s0_ref[...]

    ti = lax.broadcasted_iota(jnp.int32, (C, C), 0)
    si = lax.broadcasted_iota(jnp.int32, (C, C), 1)
    dist = (ti - si).astype(F32)
    t1 = (lax.broadcasted_iota(jnp.int32, (C, 1), 0) + 1).astype(F32)

    for h in range(H_B):
        sl = slice(h * LANES, (h + 1) * LANES)
        log_gamma = math.log1p(-(2.0 ** (-5.0 - h)))
        q = q_ref[:, sl]
        k = k_ref[:, sl]
        v16 = v_ref[:, sl].astype(BF16)
        decay = jnp.exp(jnp.where(ti >= si, dist * log_gamma, -jnp.inf))
        a = _dot_nt(q.astype(BF16), k.astype(BF16)) * decay
        s = s_ref[h]
        o = _dot(a.astype(BF16), v16) + _dot((q * jnp.exp(t1 * log_gamma)).astype(BF16), s.astype(BF16))
        kd = (k * jnp.exp((C - t1) * log_gamma)).astype(BF16)
        s_ref[h] = math.exp(C * log_gamma) * s + _dot_tn(kd, v16)
        mu = jnp.mean(o, -1, keepdims=True)
        d = o - mu
        var = jnp.mean(d * d, -1, keepdims=True)
        y = d * lax.rsqrt(var + EPS) * g_ref[...] * gate_ref[:, sl]
        y_ref[:, sl] = y.astype(BF16)

    @pl.when(c == pl.num_programs(1) - 1)
    def _():
        s_out_ref[...] = s_ref[...]


def _ret(P, s0, g, off, B, T, M, y_prev):
    C = min(T, 128)
    nc = T // C
    base = off // C
    width = H_B * DV_B

    def rows(colblk):
        return pl.BlockSpec((C, width), lambda b, c: (base + b * nc + c, colblk))

    a_specs, a_args, aliases = _y_alias(y_prev, 6)
    return pl.pallas_call(
        _ret_kernel,
        grid=(B, nc),
        in_specs=[rows(4), rows(5), rows(6), rows(7),
                  pl.BlockSpec((None, H_B, DK_B, DV_B), lambda b, c: (b, 0, 0, 0)),
                  pl.BlockSpec((1, DV_B), lambda b, c: (0, 0))] + a_specs,
        out_specs=(pl.BlockSpec((C, width), lambda b, c: (base + b * nc + c, 0)),
                   pl.BlockSpec((None, H_B, DK_B, DV_B), lambda b, c: (b, 0, 0, 0))),
        out_shape=(jax.ShapeDtypeStruct((M, width), BF16),
                   jax.ShapeDtypeStruct((B, H_B, DK_B, DV_B), F32)),
        scratch_shapes=[pltpu.VMEM((H_B, DK_B, DV_B), F32)],
        input_output_aliases=aliases,
        compiler_params=_cparams(("parallel", "arbitrary")),
        name="retention",
    )(P, P, P, P, s0, g, *a_args)


def _past_blocks(n, cap=512):
    out, r = [], 0
    while r < n:
        w = min(cap, n - r)
        out.append((r, w))
        r += w
    return out


def _colmax(mrun, s):
    n = s.shape[1]
    if n % LANES:
        return jnp.maximum(mrun, jnp.max(s, -1, keepdims=True))
    for gi in range(n // LANES):
        mrun = jnp.maximum(mrun, s[:, gi * LANES:(gi + 1) * LANES])
    return mrun


def _attend(rows, score_fn, value_fn, n_past, bq, qi, bias, sp_ref, ss_ref, acc_ref):
    blocks = _past_blocks(n_past)
    mrun = jnp.full((rows, LANES), NEG_BIG, F32)
    for r0, n in blocks:
        s = score_fn("past", r0, n)
        sp_ref[:, r0:r0 + n] = s
        mrun = _colmax(mrun, s)

    def scores(j, mrun):
        s = score_fn("self", pl.multiple_of(j * bq, bq), bq)
        ss_ref[j] = s
        return _colmax(mrun, s)

    mrun = lax.fori_loop(0, qi, scores, mrun)
    s = score_fn("self", pl.multiple_of(qi * bq, bq), bq)
    if bias is not None:
        s = s + bias
    ss_ref[qi] = s
    mrun = _colmax(mrun, s)
    m = jnp.max(mrun, -1, keepdims=True)

    acc_ref[...] = jnp.zeros_like(acc_ref)

    def accumulate(s, v16):
        p = jnp.exp2(s - m).astype(BF16)
        acc_ref[...] += _dot(p, jnp.concatenate([v16, jnp.ones_like(v16)], 1))

    for r0, n in blocks:
        accumulate(sp_ref[:, r0:r0 + n], value_fn("past", r0, n))

    def weighted(j, carry):
        accumulate(ss_ref[j], value_fn("self", pl.multiple_of(j * bq, bq), bq))
        return carry

    lax.fori_loop(0, qi, weighted, 0)
    accumulate(ss_ref[qi], value_fn("self", pl.multiple_of(qi * bq, bq), bq))
    dv = acc_ref.shape[1] // 2
    return acc_ref[:, :dv] / acc_ref[:, dv:]


def _chunk_bias(rows, bq):
    if bq <= CHUNK:
        return None
    qc = (lax.broadcasted_iota(jnp.int32, (rows, bq), 0) % bq) // CHUNK
    kc = lax.broadcasted_iota(jnp.int32, (rows, bq), 1) // CHUNK
    return jnp.where(kc <= qc, 0.0, -jnp.inf).astype(F32)


def _attn_scratch(rows, n_past, nq, bq, dv):
    return [pltpu.VMEM((rows, max(LANES, -(-n_past // LANES) * LANES)), F32),
            pltpu.VMEM((nq, rows, bq), F32),
            pltpu.VMEM((rows, 2 * dv), F32)]


def _diff_kernel(*refs, n_past, lam_init):
    refs = list(refs)
    sp_ref, ss_ref, acc_ref = refs[-3:]
    y_ref = refs[-4]
    if n_past:
        q_ref, ks_ref, vs_ref, kp_ref, vp_ref, lam_ref, g_ref = refs[:7]
    else:
        q_ref, ks_ref, vs_ref, lam_ref, g_ref = refs[:5]
        kp_ref = vp_ref = None
    qi = pl.program_id(1)
    bq = q_ref.shape[0]
    lam_p = lam_ref[...]
    lam = (jnp.exp(jnp.sum(lam_p[0:1] * lam_p[1:2], -1, keepdims=True))
           - jnp.exp(jnp.sum(lam_p[2:3] * lam_p[3:4], -1, keepdims=True)) + lam_init)
    lane = lax.broadcasted_iota(jnp.int32, (bq, LANES), 1)
    bias = _chunk_bias(2 * bq, bq)
    scale = DH_C ** -0.5 * LOG2E
    for h in range(H_C):
        sl = slice(h * LANES, (h + 1) * LANES)
        qh = q_ref[:, sl]
        zero = jnp.zeros_like(qh)
        q2 = jnp.concatenate([jnp.where(lane < DH_C, qh, zero), jnp.where(lane >= DH_C, qh, zero)], 0)

        def score_fn(src, r0, n):
            k = (kp_ref if src == "past" else ks_ref)[pl.ds(r0, n), sl]
            return _dot_nt(q2, k.astype(BF16)) * scale

        def value_fn(src, r0, n):
            return (vp_ref if src == "past" else vs_ref)[pl.ds(r0, n), sl].astype(BF16)

        a = _attend(2 * bq, score_fn, value_fn, n_past, bq, qi, bias, sp_ref, ss_ref, acc_ref)
        d = a[:bq] - lam * a[bq:]
        y_ref[:, sl] = (_rms_norm(d, g_ref[...]) * (1.0 - lam_init)).astype(BF16)


def _q_block(T):
    for bq in (512, 256, 128):
        if T % bq == 0:
            return bq
    return T


def _diff_attn(P16, past_k, past_v, lam_p, g, off, B, T, M, lam_init, y_prev):
    bq = _q_block(T)
    nq = T // bq
    width = H_C * DV_C
    in_specs = [
        pl.BlockSpec((bq, width), lambda b, i: (off // bq + b * nq + i, 0)),
        pl.BlockSpec((T, width), lambda b, i: (off // T + b, 1)),
        pl.BlockSpec((T, width), lambda b, i: (off // T + b, 2)),
    ]
    args = [P16, P16, P16]
    n_past = 0
    if past_k is not None:
        if isinstance(past_k, tuple):
            _, poff, n_past = past_k
            in_specs += [pl.BlockSpec((n_past, width), lambda b, i: (poff // n_past + b, 1)),
                         pl.BlockSpec((n_past, width), lambda b, i: (poff // n_past + b, 2))]
            args += [P16, P16]
        else:
            n_past = past_k.shape[1]
            in_specs += [pl.BlockSpec((None, n_past, width), lambda b, i: (b, 0, 0)),
                         pl.BlockSpec((None, n_past, width), lambda b, i: (b, 0, 0))]
            args += [past_k, past_v]
    in_specs += [pl.BlockSpec((4, DH_C), lambda b, i: (0, 0)), pl.BlockSpec((1, DV_C), lambda b, i: (0, 0))]
    args += [lam_p, g]
    a_specs, a_args, aliases = _y_alias(y_prev, len(args))
    return pl.pallas_call(
        functools.partial(_diff_kernel, n_past=n_past, lam_init=lam_init),
        grid=(B, nq),
        in_specs=in_specs + a_specs,
        out_specs=pl.BlockSpec((bq, width), lambda b, i: (off // bq + b * nq + i, 0)),
        out_shape=jax.ShapeDtypeStruct((M, width), BF16),
        scratch_shapes=_attn_scratch(2 * bq, n_past, nq, bq, DV_C),
        input_output_aliases=aliases,
        compiler_params=_cparams(("parallel", "arbitrary")),
        name="diff_attn",
    )(*args, *a_args)


def _mla_kernel(*refs, n_past):
    refs = list(refs)
    sp_ref, ss_ref, acc_ref = refs[-3:]
    y_ref = refs[-4]
    if n_past:
        q_ref, kvs_ref, krs_ref, kvp_ref, krp_ref = refs[:5]
    else:
        q_ref, kvs_ref, krs_ref = refs[:3]
        kvp_ref = krp_ref = None
    qi = pl.program_id(1)
    bq = q_ref.shape[0]
    bias = _chunk_bias(bq, bq)
    scale = (D_NOPE + D_ROPE) ** -0.5 * LOG2E
    for h in range(H_D):
        qn = q_ref[:, h * D_NOPE:(h + 1) * D_NOPE]
        qr = q_ref[:, H_D * D_NOPE + h * LANES:H_D * D_NOPE + (h + 1) * LANES]
        ksl = slice(h * (D_NOPE + DV_D), h * (D_NOPE + DV_D) + D_NOPE)
        vsl = slice(h * (D_NOPE + DV_D) + D_NOPE, (h + 1) * (D_NOPE + DV_D))

        def score_fn(src, r0, n):
            kv = kvp_ref if src == "past" else kvs_ref
            kr = krp_ref if src == "past" else krs_ref
            s = _dot_nt(qn, kv[pl.ds(r0, n), ksl]) + _dot_nt(qr, kr[pl.ds(r0, n), :].astype(BF16))
            return s * scale

        def value_fn(src, r0, n):
            return (kvp_ref if src == "past" else kvs_ref)[pl.ds(r0, n), vsl]

        a = _attend(bq, score_fn, value_fn, n_past, bq, qi, bias, sp_ref, ss_ref, acc_ref)
        y_ref[:, h * DV_D:(h + 1) * DV_D] = a.astype(BF16)


def _mla_attn(q16, kv16, P16, past_kv, past_kr, off, B, T, M, y_prev):
    bq = _q_block(T)
    nq = T // bq
    wq = q16.shape[1]
    wkv = kv16.shape[1]
    kr_blk = ((12 - P16_FIRST) * PROJ_TILE + KV_LORA) // LANES
    in_specs = [
        pl.BlockSpec((bq, wq), lambda b, i: (off // bq + b * nq + i, 0)),
        pl.BlockSpec((T, wkv), lambda b, i: (off // T + b, 0)),
        pl.BlockSpec((T, LANES), lambda b, i: (off // T + b, kr_blk)),
    ]
    args = [q16, kv16, P16]
    n_past = 0
    if past_kv is not None:
        if isinstance(past_kv, tuple):
            _, poff, n_past = past_kv
            in_specs += [pl.BlockSpec((n_past, wkv), lambda b, i: (poff // n_past + b, 0)),
                         pl.BlockSpec((n_past, LANES), lambda b, i: (poff // n_past + b, kr_blk))]
            args += [kv16, P16]
        else:
            n_past = past_kv.shape[1]
            in_specs += [pl.BlockSpec((None, n_past, wkv), lambda b, i: (b, 0, 0)),
                         pl.BlockSpec((None, n_past, LANES), lambda b, i: (b, 0, 0))]
            args += [past_kv, past_kr]
    width = H_D * DV_D
    a_specs, a_args, aliases = _y_alias(y_prev, len(args))
    return pl.pallas_call(
        functools.partial(_mla_kernel, n_past=n_past),
        grid=(B, nq),
        in_specs=in_specs + a_specs,
        out_specs=pl.BlockSpec((bq, width), lambda b, i: (off // bq + b * nq + i, 0)),
        out_shape=jax.ShapeDtypeStruct((M, width), BF16),
        scratch_shapes=_attn_scratch(bq, n_past, nq, bq, DV_D),
        input_output_aliases=aliases,
        compiler_params=_cparams(("parallel", "arbitrary")),
        name="mla_attn",
    )(*args, *a_args)


def _rope_table(pos, period, half, rot_dim, theta):
    lane = np.arange(LANES)
    li = lane % period
    first = li < half
    second = (li >= half) & (li < rot_dim)
    idx = np.where(first, li, np.where(second, li - half, 0))
    freq = jnp.power(jnp.float32(theta), -jnp.arange(half, dtype=F32) / half)[idx]
    ang = pos.astype(F32)[:, None] * freq[None, :]
    cos, sin = jnp.cos(ang), jnp.sin(ang)
    rot = jnp.asarray(first | second)[None, :]
    return [jnp.where(rot, cos, 1.0), jnp.where(jnp.asarray(second)[None, :], sin, 0.0),
            jnp.where(jnp.asarray(first)[None, :], -sin, 0.0)]


def _rope_tables(pos):
    tabs = (_rope_table(pos, LANES, DK_B // 2, DK_B, RET_THETA)
            + _rope_table(pos, DH_C, PARTIAL_ROT // 2, PARTIAL_ROT, ROPE_THETA)
            + _rope_table(pos, LANES, D_ROPE // 2, D_ROPE, ROPE_THETA))
    return jnp.concatenate(tabs, axis=1)


def _uq_layout(w_uq):
    depth = w_uq.shape[0]
    w = w_uq.reshape(depth, Q_LORA, H_D, D_NOPE + D_ROPE)
    nope = w[..., :D_NOPE].reshape(depth, Q_LORA, H_D * D_NOPE)
    rope = jnp.pad(w[..., D_NOPE:], ((0, 0), (0, 0), (0, 0), (0, LANES - D_ROPE))).reshape(depth, Q_LORA, H_D * LANES)
    return jnp.concatenate([nope, rope], axis=2).astype(BF16)


def kernel(x_prompt, x_sample, cache_diff_k, cache_diff_v, cache_mla_ckv, cache_mla_krope, state_hgrn, state_ret, meta_tokens, w_ffn1_in, w_ffn1_out, ln1_g, ln1_b, w_in, lb_logits, hgrn_norm_g, ret_norm_g, diff_lambda_q1, diff_lambda_k1, diff_lambda_q2, diff_lambda_k2, diff_norm_g, mla_q_norm_g, mla_kv_norm_g, w_mla_uq, w_mla_ukv, w_branch, w_merge_gate, w_mix_out, ln2_g, ln2_b, w_ffn2_in, w_ffn2_out, ln3_g, ln3_b):
    depth = w_in.shape[0]
    Bp, S, D = x_prompt.shape
    Bs, Ts, _ = x_sample.shape
    n_cache = cache_diff_k.shape[2]
    past = n_cache - N_META
    assert S % (2 * CHUNK) == 0 and Ts <= CHUNK and past % CHUNK == 0 and (past + Ts - 1) // CHUNK == past // CHUNK
    alpha = (2 * depth) ** 0.25

    off_s = Bp * S
    off_m = off_s + Bs * Ts
    M = off_m + Bp * N_META
    assert off_s % Ts == 0 and off_m % N_META == 0
    x = jnp.concatenate([
        x_prompt.reshape(Bp * S, D), x_sample.reshape(Bs * Ts, D),
        jnp.broadcast_to(meta_tokens[None].astype(x_prompt.dtype), (Bp, N_META, D)).reshape(Bp * N_META, D)], 0)
    pos = jnp.concatenate([
        jnp.tile(N_META + jnp.arange(S, dtype=jnp.int32), Bp),
        jnp.tile(N_META + past + jnp.arange(Ts, dtype=jnp.int32), Bs),
        jnp.tile(jnp.arange(N_META, dtype=jnp.int32), Bp)])
    tab = _rope_tables(pos)

    w1i, w1o = w_ffn1_in.astype(BF16), w_ffn1_out.astype(BF16)
    w2i, w2o = w_ffn2_in.astype(BF16), w_ffn2_out.astype(BF16)
    w_in16 = jnp.pad(w_in, ((0, 0), (0, 0), (0, PROJ_COLS - w_in.shape[2]))).astype(BF16)
    w_uq16 = _uq_layout(w_mla_uq)
    w_ukv16 = w_mla_ukv.astype(BF16)
    w_gate16, w_branch16, w_out16 = w_merge_gate.astype(BF16), w_branch.astype(BF16), w_mix_out.astype(BF16)

    la, lc = _lower_bounds(lb_logits)
    groups = (("meta", off_m, Bp, N_META), ("prompt", 0, Bp, S), ("sample", off_s, Bs, Ts))
    row = lambda a: a.reshape(1, -1).astype(F32)
    out_dt = x_prompt.dtype

    def alloc(B, T, tail):
        return jnp.zeros((depth, B, T) + tail, out_dt)

    tails = {"k": (H_C, 2 * DH_C), "v": (H_C, DV_C), "ckv": (KV_LORA,), "kr": (D_ROPE,)}
    cols = {"k": (9 * PROJ_TILE, 10 * PROJ_TILE), "v": (10 * PROJ_TILE, 11 * PROJ_TILE),
            "ckv": (12 * PROJ_TILE, 12 * PROJ_TILE + KV_LORA),
            "kr": (12 * PROJ_TILE + KV_LORA, 12 * PROJ_TILE + KV_LORA + D_ROPE)}
    p_out = {k: alloc(Bp, N_META + S, t) for k, t in tails.items()}
    s_out = {k: alloc(Bs, Ts, t) for k, t in tails.items()}
    states = {k: [] for k in ("psa", "psb", "ssa", "ssb")}

    for l in range(depth):
        x, x16 = _ffn(x, w1i, w1o, l, row(ln1_g[l]), row(ln1_b[l]), alpha)

        lb2 = jnp.stack([la[l], lc[l]], 0)
        P, P16 = _proj(x16, w_in16, l, lb2, row(mla_q_norm_g[l]), row(mla_kv_norm_g[l]), tab)
        q_d = _q_up(P16, w_uq16, l, tab)
        kv_new = _kv_up(P16, ((12 - P16_FIRST) * PROJ_TILE) // KV_LORA, w_ukv16, l)
        kv_past = _kv_up(cache_mla_ckv[l].reshape(Bs * n_cache, KV_LORA), 0, w_ukv16, l).reshape(Bs, n_cache, -1)
        kr_past = jnp.pad(cache_mla_krope[l], ((0, 0), (0, 0), (0, LANES - D_ROPE)))
        ck_past = cache_diff_k[l].reshape(Bs, n_cache, H_C * 2 * DH_C)
        cv_past = cache_diff_v[l].reshape(Bs, n_cache, H_C * DV_C)
        lam_p = jnp.stack([diff_lambda_q1[l], diff_lambda_k1[l], diff_lambda_q2[l], diff_lambda_k2[l]], 0).astype(F32)
        lam_init = 0.8 - 0.6 * math.exp(-0.3 * l)

        ya = yb = yc = yd = None
        sa_meta = sb_meta = None
        for name, off, B, T in groups:
            if name == "meta":
                sa0 = jnp.zeros((B, H_A, DK_A, DV_A), F32)
                sb0 = jnp.zeros((B, H_B, DK_B, DV_B), F32)
                past_c = past_d = (None, None)
            elif name == "prompt":
                sa0, sb0 = sa_meta, sb_meta
                past_c = past_d = (("flat", off_m, N_META),) * 2
            else:
                sa0, sb0 = state_hgrn[l].astype(F32), state_ret[l].astype(F32)
                past_c, past_d = (ck_past, cv_past), (kv_past, kr_past)
            ya, sa = _hgrn(P, sa0, row(hgrn_norm_g[l]), off, B, T, M, ya)
            yb, sb = _ret(P, sb0, row(ret_norm_g[l]), off, B, T, M, yb)
            yc = _diff_attn(P16, past_c[0], past_c[1], lam_p, row(diff_norm_g[l]), off, B, T, M, lam_init, yc)
            yd = _mla_attn(q_d, kv_new, P16, past_d[0], past_d[1], off, B, T, M, yd)
            if name == "meta":
                sa_meta, sb_meta = sa, sb
            elif name == "prompt":
                states["psa"].append(sa); states["psb"].append(sb)
            else:
                states["ssa"].append(sa); states["ssb"].append(sb)

        merged = _merge(x16, (ya, yb, yc, yd), w_gate16, w_branch16, l)
        x, x16 = _mix_out(x, merged, w_out16, l, row(ln2_g[l]), row(ln2_b[l]), alpha)
        x, x16 = _ffn(x, w2i, w2o, l, row(ln3_g[l]), row(ln3_b[l]), alpha)

        for key, (c0, c1) in cols.items():
            piece = P[:, c0:c1]
            tail = tails[key]
            p_out[key] = p_out[key].at[l, :, :N_META].set(piece[off_m:].reshape((Bp, N_META) + tail))
            p_out[key] = p_out[key].at[l, :, N_META:].set(piece[:off_s].reshape((Bp, S) + tail))
            s_out[key] = s_out[key].at[l].set(piece[off_s:off_m].reshape((Bs, Ts) + tail))

    y_prompt = x[:off_s].reshape(Bp, S, D)
    y_sample = x[off_s:off_m].reshape(Bs, Ts, D)
    st = lambda k: jnp.stack(states[k], 0).astype(out_dt)
    return (y_prompt, y_sample, p_out["k"], p_out["v"], p_out["ckv"], p_out["kr"], st("psa"), st("psb"),
            s_out["k"], s_out["v"], s_out["ckv"], s_out["kr"], st("ssa"), st("ssb"))
```

```python
import functools
import math

import numpy as np
import jax
import jax.numpy as jnp
from jax import lax
from jax.experimental import pallas as pl
from jax.experimental.pallas import tpu as pltpu

F32 = jnp.float32
BF16 = jnp.bfloat16

D_MODEL = 2048
CHUNK = 64
N_META = 16
N_BRANCH = 4
BRANCH_W = 512
H_A, DK_A, DV_A = 4, 128, 128
H_B, DK_B, DV_B = 4, 128, 128
RET_THETA = 10000.0
H_C, DH_C, DV_C = 4, 64, 128
PARTIAL_ROT = DH_C // 4
H_D = 4
Q_LORA, KV_LORA = 512, 256
D_NOPE, D_ROPE, DV_D = 128, 64, 128
D_FF = 5632
ROPE_THETA = 500000.0
EPS = 1e-5

LANES = 128
SUBLANES = 8
PROJ_TILE = 512
PROJ_COLS = 13 * PROJ_TILE
P16_FIRST = 8
P16_COLS = PROJ_COLS - P16_FIRST * PROJ_TILE
PA_TILES = 8
VMEM_LIMIT = 56 * 1024 * 1024
NEG_BIG = -1e30
LOG2E = math.log2(math.e)


def _cparams(sem):
    return pltpu.CompilerParams(dimension_semantics=sem, vmem_limit_bytes=VMEM_LIMIT)


def _pick_tile(n, cap, mult=16):
    best = None
    for t in range(mult, min(n, cap) + 1, mult):
        if n % t == 0:
            best = t
    return best if best is not None else n


def _dot(a, b):
    return jnp.dot(a, b, preferred_element_type=F32)


def _dot_nt(a, b):
    return lax.dot_general(a, b, (((1,), (1,)), ((), ())), preferred_element_type=F32)


def _dot_tn(a, b):
    return lax.dot_general(a, b, (((0,), (0,)), ((), ())), preferred_element_type=F32)


def _layer_norm(z, g, b):
    mu = jnp.mean(z, -1, keepdims=True)
    d = z - mu
    var = jnp.mean(d * d, -1, keepdims=True)
    return d * lax.rsqrt(var + EPS) * g + b


def _rms_norm(z, g):
    return z * lax.rsqrt(jnp.mean(z * z, -1, keepdims=True) + EPS) * g


def _silu(a):
    return a * jax.nn.sigmoid(a)


def _rope128(x, c, s1, s2, shift):
    return x * c + pltpu.roll(x, shift, 1) * s1 + pltpu.roll(x, LANES - shift, 1) * s2


def _lower_bound_kernel(logit_ref, la_ref, lc_ref):
    z = logit_ref[...]
    depth = z.shape[0]
    m = z[0:1]
    for l in range(1, depth):
        m = jnp.maximum(m, z[l:l + 1])
    e = jnp.exp(z - m)
    tot = e[0:1]
    for l in range(1, depth):
        tot = tot + e[l:l + 1]
    p = e / tot
    run = jnp.zeros_like(m)
    for l in range(depth):
        la_ref[l:l + 1, :] = jnp.log(run)
        lc_ref[l:l + 1, :] = jnp.log1p(-run)
        run = run + p[l:l + 1]


def _lower_bounds(lb_logits):
    shp = jax.ShapeDtypeStruct(lb_logits.shape, F32)
    return pl.pallas_call(_lower_bound_kernel, out_shape=(shp, shp), name="lower_bounds")(
        lb_logits.astype(F32))


def _ffn_kernel(x_ref, wa_ref, wb_ref, wo_ref, g_ref, b_ref, y32_ref, y16_ref, xs_ref, acc_ref, *, alpha):
    f = pl.program_id(1)

    @pl.when(f == 0)
    def _():
        xs_ref[...] = x_ref[...].astype(BF16)
        acc_ref[...] = jnp.zeros_like(acc_ref)

    x = xs_ref[...]
    a = _dot(x, wa_ref[...])
    b = _dot(x, wb_ref[...])
    h = (_silu(a) * b).astype(BF16)
    acc_ref[...] += _dot(h, wo_ref[...])

    @pl.when(f == pl.num_programs(1) - 1)
    def _():
        y = _layer_norm(alpha * x_ref[...] + 0.5 * acc_ref[...], g_ref[...], b_ref[...])
        y32_ref[...] = y
        y16_ref[...] = y.astype(BF16)


def _ffn(x32, w_in16, w_out16, l, g, b, alpha):
    M, D = x32.shape
    F = w_out16.shape[1]
    tm = _pick_tile(M, 640)
    tf = _pick_tile(F, 512, LANES)
    nf = F // tf
    return pl.pallas_call(
        functools.partial(_ffn_kernel, alpha=alpha),
        grid=(M // tm, nf),
        in_specs=[
            pl.BlockSpec((tm, D), lambda i, f: (i, 0)),
            pl.BlockSpec((None, D, tf), lambda i, f: (l, 0, f)),
            pl.BlockSpec((None, D, tf), lambda i, f: (l, 0, nf + f)),
            pl.BlockSpec((None, tf, D), lambda i, f: (l, f, 0)),
            pl.BlockSpec((1, D), lambda i, f: (0, 0)),
            pl.BlockSpec((1, D), lambda i, f: (0, 0)),
        ],
        out_specs=(pl.BlockSpec((tm, D), lambda i, f: (i, 0)),
                   pl.BlockSpec((tm, D), lambda i, f: (i, 0))),
        out_shape=(jax.ShapeDtypeStruct((M, D), F32), jax.ShapeDtypeStruct((M, D), BF16)),
        scratch_shapes=[pltpu.VMEM((tm, D), BF16), pltpu.VMEM((tm, D), F32)],
        compiler_params=_cparams(("parallel", "arbitrary")),
        name="ffn",
    )(x32, w_in16, w_in16, w_out16, g, b)


def _proj_kernel(x_ref, w_ref, lb_ref, qg_ref, kvg_ref, tab_ref, pa_ref, kc_ref, vc_ref, ckr_ref, o16_ref):
    j = pl.program_id(1)
    tm = x_ref.shape[0]

    def acc():
        return _dot(x_ref[...], w_ref[...])

    def tab(k):
        return tab_ref[:, k * LANES:(k + 1) * LANES]

    def rope_heads(t0, shift):
        c, s1, s2 = tab(t0), tab(t0 + 1), tab(t0 + 2)
        a = acc()
        return [_rope128(a[:, h * LANES:(h + 1) * LANES], c, s1, s2, shift) for h in range(PROJ_TILE // LANES)]

    @pl.when((j == 0) | (j == 3) | (j == 7))
    def _():
        pa_ref[...] = _silu(acc())

    @pl.when(j == 1)
    def _():
        z = acc()
        log_sig = -(jnp.maximum(-z, 0.0) + jnp.log1p(jnp.exp(-jnp.abs(z))))
        a = lb_ref[0:1, :]
        c = lb_ref[1:2, :] + log_sig
        pa_ref[...] = jnp.maximum(a, c) + jnp.log1p(jnp.exp(-jnp.abs(a - c)))

    @pl.when((j == 2) | (j == 6))
    def _():
        pa_ref[...] = acc()

    @pl.when(j == 4)
    def _():
        for h, r in enumerate(rope_heads(0, DK_B // 2)):
            pa_ref[:, h * LANES:(h + 1) * LANES] = r

    @pl.when(j == 5)
    def _():
        for h, r in enumerate(rope_heads(0, DK_B // 2)):
            pa_ref[:, h * LANES:(h + 1) * LANES] = r * DK_B ** -0.5

    @pl.when(j == 8)
    def _():
        for h, r in enumerate(rope_heads(3, PARTIAL_ROT // 2)):
            o16_ref[:, h * LANES:(h + 1) * LANES] = r.astype(BF16)

    @pl.when(j == 9)
    def _():
        for h, r in enumerate(rope_heads(3, PARTIAL_ROT // 2)):
            kc_ref[pl.ds(h, tm, stride=H_C), :] = r
            o16_ref[:, h * LANES:(h + 1) * LANES] = r.astype(BF16)

    @pl.when(j == 10)
    def _():
        a = acc()
        for h in range(H_C):
            vc_ref[pl.ds(h, tm, stride=H_C), :] = a[:, h * LANES:(h + 1) * LANES]
        o16_ref[...] = a.astype(BF16)

    @pl.when(j == 11)
    def _():
        o16_ref[...] = _rms_norm(acc(), qg_ref[...]).astype(BF16)

    @pl.when(j == 12)
    def _():
        a = acc()
        ckv = _rms_norm(a[:, :KV_LORA], kvg_ref[...])
        kr = _rope128(a[:, KV_LORA:KV_LORA + LANES], tab(6), tab(7), tab(8), D_ROPE // 2)
        zeros = jnp.zeros((tm, PROJ_TILE - KV_LORA - LANES), F32)
        ckr_ref[:, :KV_LORA] = ckv
        ckr_ref[:, KV_LORA:KV_LORA + LANES] = kr
        ckr_ref[:, KV_LORA + LANES:] = zeros
        o16_ref[:, :KV_LORA] = ckv.astype(BF16)
        o16_ref[:, KV_LORA:KV_LORA + LANES] = kr.astype(BF16)
        o16_ref[:, KV_LORA + LANES:] = zeros.astype(BF16)


def _proj(x16, w16, l, lb2, qg, kvg, tab):
    M, D = x16.shape
    tm = _pick_tile(M, 640)
    nj = PROJ_COLS // PROJ_TILE
    once = lambda i, j: (i, 0)
    return pl.pallas_call(
        _proj_kernel,
        grid=(M // tm, nj),
        in_specs=[
            pl.BlockSpec((tm, D), once),
            pl.BlockSpec((None, D, PROJ_TILE), lambda i, j: (l, 0, j)),
            pl.BlockSpec((2, PROJ_TILE), lambda i, j: (0, 0)),
            pl.BlockSpec((1, Q_LORA), lambda i, j: (0, 0)),
            pl.BlockSpec((1, KV_LORA), lambda i, j: (0, 0)),
            pl.BlockSpec((tm, 9 * LANES), once),
        ],
        out_specs=(pl.BlockSpec((tm, PROJ_TILE), lambda i, j: (i, jnp.minimum(j, PA_TILES - 1))),
                   pl.BlockSpec((tm * H_C, LANES), once),
                   pl.BlockSpec((tm * H_C, LANES), once),
                   pl.BlockSpec((tm, PROJ_TILE), once),
                   pl.BlockSpec((tm, PROJ_TILE), lambda i, j: (i, jnp.maximum(j - P16_FIRST, 0)))),
        out_shape=(jax.ShapeDtypeStruct((M, PA_TILES * PROJ_TILE), F32),
                   jax.ShapeDtypeStruct((M * H_C, LANES), F32),
                   jax.ShapeDtypeStruct((M * H_C, LANES), F32),
                   jax.ShapeDtypeStruct((M, PROJ_TILE), F32),
                   jax.ShapeDtypeStruct((M, P16_COLS), BF16)),
        compiler_params=_cparams(("parallel", "arbitrary")),
        name="proj",
    )(x16, w16, lb2, qg, kvg, tab)


def _q_up_kernel(x_ref, w_ref, tab_ref, o_ref):
    acc = _dot(x_ref[...], w_ref[...])
    nope = H_D * D_NOPE
    o_ref[:, :nope] = acc[:, :nope].astype(BF16)
    c, s1, s2 = (tab_ref[:, k * LANES:(k + 1) * LANES] for k in range(3))
    for h in range(H_D):
        lo = nope + h * LANES
        o_ref[:, lo:lo + LANES] = _rope128(acc[:, lo:lo + LANES], c, s1, s2, D_ROPE // 2).astype(BF16)


def _q_up(P16, w16, l, tab):
    M = P16.shape[0]
    N = w16.shape[2]
    tm = _pick_tile(M, 640)
    return pl.pallas_call(
        _q_up_kernel,
        grid=(M // tm,),
        in_specs=[
            pl.BlockSpec((tm, Q_LORA), lambda i: (i, 11 - P16_FIRST)),
            pl.BlockSpec((None, Q_LORA, N), lambda i: (l, 0, 0)),
            pl.BlockSpec((tm, 3 * LANES), lambda i: (i, 2)),
        ],
        out_specs=pl.BlockSpec((tm, N), lambda i: (i, 0)),
        out_shape=jax.ShapeDtypeStruct((M, N), BF16),
        compiler_params=_cparams(("parallel",)),
        name="q_up",
    )(P16, w16, tab)


def _kv_up_kernel(x_ref, w_ref, o_ref):
    o_ref[...] = _dot(x_ref[...].astype(BF16), w_ref[...]).astype(BF16)


def _kv_up(x, col_block, w16, l, row0=0, n_rows=None):
    M = x.shape[0] if n_rows is None else n_rows
    N = w16.shape[2]
    tm = _pick_tile(math.gcd(M, row0) if row0 else M, 1024)
    blk0 = row0 // tm
    return pl.pallas_call(
        _kv_up_kernel,
        grid=(M // tm,),
        in_specs=[
            pl.BlockSpec((tm, KV_LORA), lambda i: (blk0 + i, col_block)),
            pl.BlockSpec((None, KV_LORA, N), lambda i: (l, 0, 0)),
        ],
        out_specs=pl.BlockSpec((tm, N), lambda i: (i, 0)),
        out_shape=jax.ShapeDtypeStruct((M, N), BF16),
        compiler_params=_cparams(("parallel",)),
        name="kv_up",
    )(x, w16)


def _merge_kernel(h_ref, y_ref, wg_ref, wb_ref, o_ref, acc_ref):
    n = pl.program_id(2)

    @pl.when(n == 0)
    def _():
        acc_ref[...] = jnp.zeros_like(acc_ref)

    gate = jax.nn.sigmoid(_dot(h_ref[...], wg_ref[...]))
    acc_ref[...] += gate * _dot(y_ref[...], wb_ref[...])

    @pl.when(n == pl.num_programs(2) - 1)
    def _():
        o_ref[...] = acc_ref[...].astype(BF16)


def _merge(h16, y, wg16, wb16, l):
    M, D = h16.shape
    W = BRANCH_W
    tm = _pick_tile(M, 640)
    tn = 512
    return pl.pallas_call(
        _merge_kernel,
        grid=(M // tm, D // tn, N_BRANCH),
        in_specs=[
            pl.BlockSpec((tm, D), lambda i, j, n: (i, 0)),
            pl.BlockSpec((tm, W), lambda i, j, n: (i, n)),
            pl.BlockSpec((None, None, D, tn), lambda i, j, n: (l, n, 0, j)),
            pl.BlockSpec((None, None, W, tn), lambda i, j, n: (l, n, 0, j)),
        ],
        out_specs=pl.BlockSpec((tm, tn), lambda i, j, n: (i, j)),
        out_shape=jax.ShapeDtypeStruct((M, D), BF16),
        scratch_shapes=[pltpu.VMEM((tm, tn), F32)],
        compiler_params=_cparams(("parallel", "arbitrary", "arbitrary")),
        name="merge",
    )(h16, y, wg16, wb16)


def _mix_out_kernel(x_ref, m_ref, w_ref, g_ref, b_ref, y32_ref, y16_ref, *, alpha):
    y = _layer_norm(alpha * x_ref[...] + _dot(m_ref[...], w_ref[...]), g_ref[...], b_ref[...])
    y32_ref[...] = y
    y16_ref[...] = y.astype(BF16)


def _mix_out(x32, merged16, w16, l, g, b, alpha):
    M, D = x32.shape
    tm = _pick_tile(M, 320)
    return pl.pallas_call(
        functools.partial(_mix_out_kernel, alpha=alpha),
        grid=(M // tm,),
        in_specs=[
            pl.BlockSpec((tm, D), lambda i: (i, 0)),
            pl.BlockSpec((tm, D), lambda i: (i, 0)),
            pl.BlockSpec((None, D, D), lambda i: (l, 0, 0)),
            pl.BlockSpec((1, D), lambda i: (0, 0)),
            pl.BlockSpec((1, D), lambda i: (0, 0)),
        ],
        out_specs=(pl.BlockSpec((tm, D), lambda i: (i, 0)),
                   pl.BlockSpec((tm, D), lambda i: (i, 0))),
        out_shape=(jax.ShapeDtypeStruct((M, D), F32), jax.ShapeDtypeStruct((M, D), BF16)),
        compiler_params=_cparams(("parallel",)),
        name="mix_out",
    )(x32, merged16, w16, g, b)


def _y_alias(y_prev, n_inputs):
    if y_prev is None:
        return [], [], {}
    return [pl.BlockSpec(memory_space=pl.ANY)], [y_prev], {n_inputs: 0}


def _state_spec(s0, layer, H, DK, DV):
    if layer is None:
        return pl.BlockSpec((None, H, DK, DV), lambda b, c: (b, 0, 0, 0))
    return pl.BlockSpec((None, None, H, DK, DV), lambda b, c: (layer, b, 0, 0, 0))


def _hgrn_kernel(*refs, C, has_s0):
    q_ref, lf_ref, v_ref, gate_ref = refs[:4]
    s0_ref = refs[4] if has_s0 else None
    g_ref = refs[4 + has_s0]
    y_ref, s_out_ref, st_ref = refs[-3:]
    step = pl.program_id(1)
    n_sub = q_ref.shape[0] // C
    nv = C // SUBLANES

    @pl.when(step == 0)
    def _():
        for h in range(H_A):
            st_ref[h] = s0_ref[h].T if has_s0 else jnp.zeros((DV_A, DK_A), F32)

    row8 = lax.broadcasted_iota(jnp.int32, (SUBLANES, LANES), 0)
    lane8 = lax.broadcasted_iota(jnp.int32, (SUBLANES, C), 1)
    tril = (lax.broadcasted_iota(jnp.int32, (C, C), 1) <= lax.broadcasted_iota(jnp.int32, (C, C), 0)).astype(F32)

    def chunk(ci, carry):
        rs = pl.ds(pl.multiple_of(ci * C, C), C)
        for h in range(H_A):
            sl = slice(h * LANES, (h + 1) * LANES)
            q = q_ref[rs, sl]
            g = lf_ref[rs, sl]
            v16 = v_ref[rs, sl].astype(BF16)
            k = 1.0 - jnp.exp(g)
            b = jnp.dot(tril, g, preferred_element_type=F32, precision=lax.Precision.HIGHEST)
            b2 = b * LOG2E
            b_last = b[C - 1:C, :]
            st = st_ref[h]
            o = _dot_nt((q * jnp.exp(b)).astype(BF16), st.astype(BF16))
            pieces = []
            for s in range(C):
                r0 = (s // SUBLANES) * SUBLANES
                d = b2[r0:] - b2[s:s + 1]
                head = jnp.where(row8 >= s % SUBLANES, d[:SUBLANES], -jnp.inf)
                d = head if C - r0 == SUBLANES else jnp.concatenate([head, d[SUBLANES:]], 0)
                pieces.append(q[r0:] * jnp.exp2(d))
            res = _dot_nt(jnp.concatenate(pieces, 0).astype(BF16), k.astype(BF16))
            a_parts = [jnp.zeros((SUBLANES, C), F32) for _ in range(nv)]
            off = 0
            for s in range(C):
                v0 = s // SUBLANES
                for i in range(v0, nv):
                    a_parts[i] = a_parts[i] + jnp.where(lane8 == s, res[off:off + SUBLANES], 0.0)
                    off += SUBLANES
            a = jnp.concatenate(a_parts, 0)
            o = o + _dot(a.astype(BF16), v16)
            kd = (k * jnp.exp(b_last - b)).astype(BF16)
            st_ref[h] = st * jnp.exp(b_last) + _dot_tn(v16, kd)
            y_ref[rs, sl] = (_rms_norm(o, g_ref[...]) * gate_ref[rs, sl]).astype(BF16)
        return carry

    lax.fori_loop(0, n_sub, chunk, 0)

    @pl.when(step == pl.num_programs(1) - 1)
    def _():
        for h in range(H_A):
            s_out_ref[h] = st_ref[h].T


def _hgrn(Pa, s0, s0_layer, g, off, B, T, M, y_prev):
    C = min(T, 64)
    rows_blk = min(T, 256)
    nb = T // rows_blk
    base = off // rows_blk
    width = H_A * DV_A

    def rows(colblk):
        return pl.BlockSpec((rows_blk, width), lambda b, c: (base + b * nb + c, colblk))

    s_specs, s_args = ([], []) if s0 is None else ([_state_spec(s0, s0_layer, H_A, DK_A, DV_A)], [s0])
    a_specs, a_args, aliases = _y_alias(y_prev, 5 + len(s_args))
    return pl.pallas_call(
        functools.partial(_hgrn_kernel, C=C, has_s0=bool(s_args)),
        grid=(B, nb),
        in_specs=[rows(0), rows(1), rows(2), rows(3)] + s_specs + [pl.BlockSpec((1, DV_A), lambda b, c: (0, 0))] + a_specs,
        out_specs=(pl.BlockSpec((rows_blk, width), lambda b, c: (base + b * nb + c, 0)),
                   pl.BlockSpec((None, H_A, DK_A, DV_A), lambda b, c: (b, 0, 0, 0))),
        out_shape=(jax.ShapeDtypeStruct((M, N_BRANCH * BRANCH_W), BF16),
                   jax.ShapeDtypeStruct((B, H_A, DK_A, DV_A), F32)),
        scratch_shapes=[pltpu.VMEM((H_A, DV_A, DK_A), F32)],
        input_output_aliases=aliases,
        compiler_params=_cparams(("parallel", "arbitrary")),
        name="hgrn",
    )(Pa, Pa, Pa, Pa, *s_args, g, *a_args)


def _ret_kernel(*refs, has_s0):
    q_ref, k_ref, v_ref, gate_ref = refs[:4]
    s0_ref = refs[4] if has_s0 else None
    g_ref = refs[4 + has_s0]
    y_ref, s_out_ref, s_ref = refs[-3:]
    c = pl.program_id(1)
    C = q_ref.shape[0]

    @pl.when(c == 0)
    def _():
        s_ref[...] = s0_ref[...] if has_s0 else jnp.zeros_like(s_ref)

    ti = lax.broadcasted_iota(jnp.int32, (C, C), 0)
    si = lax.broadcasted_iota(jnp.int32, (C, C), 1)
    dist = (ti - si).astype(F32)
    t1 = (lax.broadcasted_iota(jnp.int32, (C, 1), 0) + 1).astype(F32)

    for h in range(H_B):
        sl = slice(h * LANES, (h + 1) * LANES)
        log_gamma = math.log1p(-(2.0 ** (-5.0 - h)))
        q = q_ref[:, sl]
        k = k_ref[:, sl]
        v16 = v_ref[:, sl].astype(BF16)
        decay = jnp.exp(jnp.where(ti >= si, dist * log_gamma, -jnp.inf))
        a = _dot_nt(q.astype(BF16), k.astype(BF16)) * decay
        s = s_ref[h]
        o = _dot(a.astype(BF16), v16) + _dot((q * jnp.exp(t1 * log_gamma)).astype(BF16), s.astype(BF16))
        kd = (k * jnp.exp((C - t1) * log_gamma)).astype(BF16)
        s_ref[h] = math.exp(C * log_gamma) * s + _dot_tn(kd, v16)
        mu = jnp.mean(o, -1, keepdims=True)
        d = o - mu
        var = jnp.mean(d * d, -1, keepdims=True)
        y = d * lax.rsqrt(var + EPS) * g_ref[...] * gate_ref[:, sl]
        y_ref[:, sl] = y.astype(BF16)

    @pl.when(c == pl.num_programs(1) - 1)
    def _():
        s_out_ref[...] = s_ref[...]


def _ret(Pa, s0, s0_layer, g, off, B, T, M, y_prev):
    C = min(T, 128)
    nc = T // C
    base = off // C
    width = H_B * DV_B

    def rows(colblk):
        return pl.BlockSpec((C, width), lambda b, c: (base + b * nc + c, colblk))

    s_specs, s_args = ([], []) if s0 is None else ([_state_spec(s0, s0_layer, H_B, DK_B, DV_B)], [s0])
    a_specs, a_args, aliases = _y_alias(y_prev, 5 + len(s_args))
    return pl.pallas_call(
        functools.partial(_ret_kernel, has_s0=bool(s_args)),
        grid=(B, nc),
        in_specs=[rows(4), rows(5), rows(6), rows(7)] + s_specs + [pl.BlockSpec((1, DV_B), lambda b, c: (0, 0))] + a_specs,
        out_specs=(pl.BlockSpec((C, width), lambda b, c: (base + b * nc + c, 1)),
                   pl.BlockSpec((None, H_B, DK_B, DV_B), lambda b, c: (b, 0, 0, 0))),
        out_shape=(jax.ShapeDtypeStruct((M, N_BRANCH * BRANCH_W), BF16),
                   jax.ShapeDtypeStruct((B, H_B, DK_B, DV_B), F32)),
        scratch_shapes=[pltpu.VMEM((H_B, DK_B, DV_B), F32)],
        input_output_aliases=aliases,
        compiler_params=_cparams(("parallel", "arbitrary")),
        name="retention",
    )(Pa, Pa, Pa, Pa, *s_args, g, *a_args)


def _past_blocks(n, cap=512):
    out, r = [], 0
    while r < n:
        w = min(cap, n - r)
        out.append((r, w))
        r += w
    return out


def _colmax(mrun, s):
    n = s.shape[1]
    if n % LANES:
        return jnp.maximum(mrun, jnp.max(s, -1, keepdims=True))
    for gi in range(n // LANES):
        mrun = jnp.maximum(mrun, s[:, gi * LANES:(gi + 1) * LANES])
    return mrun


def _attend(rows, score_fn, value_fn, n_past, bq, qi, bias, sp_ref, ss_ref, acc_ref):
    blocks = _past_blocks(n_past)
    mrun = jnp.full((rows, LANES), NEG_BIG, F32)
    for r0, n in blocks:
        s = score_fn("past", r0, n)
        sp_ref[:, r0:r0 + n] = s
        mrun = _colmax(mrun, s)

    def scores(j, mrun):
        s = score_fn("self", pl.multiple_of(j * bq, bq), bq)
        ss_ref[j] = s
        return _colmax(mrun, s)

    mrun = lax.fori_loop(0, qi, scores, mrun)
    s = score_fn("self", pl.multiple_of(qi * bq, bq), bq)
    if bias is not None:
        s = s + bias
    ss_ref[qi] = s
    mrun = _colmax(mrun, s)
    m = jnp.max(mrun, -1, keepdims=True)

    acc_ref[...] = jnp.zeros_like(acc_ref)

    def accumulate(s, v16):
        p = jnp.exp2(s - m).astype(BF16)
        acc_ref[...] += _dot(p, jnp.concatenate([v16, jnp.ones_like(v16)], 1))

    for r0, n in blocks:
        accumulate(sp_ref[:, r0:r0 + n], value_fn("past", r0, n))

    def weighted(j, carry):
        accumulate(ss_ref[j], value_fn("self", pl.multiple_of(j * bq, bq), bq))
        return carry

    lax.fori_loop(0, qi, weighted, 0)
    accumulate(ss_ref[qi], value_fn("self", pl.multiple_of(qi * bq, bq), bq))
    dv = acc_ref.shape[1] // 2
    return acc_ref[:, :dv] / acc_ref[:, dv:]


def _chunk_bias(rows, bq):
    if bq <= CHUNK:
        return None
    qc = (lax.broadcasted_iota(jnp.int32, (rows, bq), 0) % bq) // CHUNK
    kc = lax.broadcasted_iota(jnp.int32, (rows, bq), 1) // CHUNK
    return jnp.where(kc <= qc, 0.0, -jnp.inf).astype(F32)


def _attn_scratch(rows, n_past, nq, bq, dv):
    return [pltpu.VMEM((rows, max(LANES, -(-n_past // LANES) * LANES)), F32),
            pltpu.VMEM((nq, rows, bq), F32),
            pltpu.VMEM((rows, 2 * dv), F32)]


def _diff_kernel(*refs, past_mode, n_past, lam_init):
    refs = list(refs)
    sp_ref, ss_ref, acc_ref = refs[-3:]
    y_ref = refs[-4]
    if past_mode:
        q_ref, ks_ref, vs_ref, kp_ref, vp_ref, lam_ref, g_ref = refs[:7]
    else:
        q_ref, ks_ref, vs_ref, lam_ref, g_ref = refs[:5]
        kp_ref = vp_ref = None
    qi = pl.program_id(1)
    bq = q_ref.shape[0]
    lam_p = lam_ref[...]
    lam = (jnp.exp(jnp.sum(lam_p[0:1] * lam_p[1:2], -1, keepdims=True))
           - jnp.exp(jnp.sum(lam_p[2:3] * lam_p[3:4], -1, keepdims=True)) + lam_init)
    lane = lax.broadcasted_iota(jnp.int32, (bq, LANES), 1)
    bias = _chunk_bias(2 * bq, bq)
    scale = DH_C ** -0.5 * LOG2E
    for h in range(H_C):
        sl = slice(h * LANES, (h + 1) * LANES)
        qh = q_ref[:, sl]
        zero = jnp.zeros_like(qh)
        q2 = jnp.concatenate([jnp.where(lane < DH_C, qh, zero), jnp.where(lane >= DH_C, qh, zero)], 0)

        def rows_of(ref, src, r0, n):
            if src == "past" and past_mode == "cache":
                return ref[pl.ds(H_C * r0 + h, n, stride=H_C), :]
            return ref[pl.ds(r0, n), sl]

        def score_fn(src, r0, n):
            k = rows_of(kp_ref if src == "past" else ks_ref, src, r0, n)
            return _dot_nt(q2, k.astype(BF16)) * scale

        def value_fn(src, r0, n):
            return rows_of(vp_ref if src == "past" else vs_ref, src, r0, n).astype(BF16)

        a = _attend(2 * bq, score_fn, value_fn, n_past, bq, qi, bias, sp_ref, ss_ref, acc_ref)
        d = a[:bq] - lam * a[bq:]
        y_ref[:, sl] = (_rms_norm(d, g_ref[...]) * (1.0 - lam_init)).astype(BF16)


def _q_block(T):
    for bq in (512, 256, 128):
        if T % bq == 0:
            return bq
    return T


def _diff_attn(P16, past, lam_p, g, off, B, T, M, lam_init, y_prev):
    bq = _q_block(T)
    nq = T // bq
    width = H_C * DV_C
    in_specs = [
        pl.BlockSpec((bq, width), lambda b, i: (off // bq + b * nq + i, 0)),
        pl.BlockSpec((T, width), lambda b, i: (off // T + b, 1)),
        pl.BlockSpec((T, width), lambda b, i: (off // T + b, 2)),
    ]
    args = [P16, P16, P16]
    n_past, past_mode = 0, None
    if past is not None:
        past_mode = past[0]
        if past_mode == "flat":
            _, poff, n_past = past
            in_specs += [pl.BlockSpec((n_past, width), lambda b, i: (poff // n_past + b, 1)),
                         pl.BlockSpec((n_past, width), lambda b, i: (poff // n_past + b, 2))]
            args += [P16, P16]
        else:
            _, pk, pv, layer = past
            n_past = pk.shape[2] // H_C
            spec = pl.BlockSpec((None, None, n_past * H_C, LANES), lambda b, i: (layer, b, 0, 0))
            in_specs += [spec, spec]
            args += [pk, pv]
    in_specs += [pl.BlockSpec((4, DH_C), lambda b, i: (0, 0)), pl.BlockSpec((1, DV_C), lambda b, i: (0, 0))]
    args += [lam_p, g]
    a_specs, a_args, aliases = _y_alias(y_prev, len(args))
    return pl.pallas_call(
        functools.partial(_diff_kernel, past_mode=past_mode, n_past=n_past, lam_init=lam_init),
        grid=(B, nq),
        in_specs=in_specs + a_specs,
        out_specs=pl.BlockSpec((bq, width), lambda b, i: (off // bq + b * nq + i, 2)),
        out_shape=jax.ShapeDtypeStruct((M, N_BRANCH * BRANCH_W), BF16),
        scratch_shapes=_attn_scratch(2 * bq, n_past, nq, bq, DV_C),
        input_output_aliases=aliases,
        compiler_params=_cparams(("parallel", "arbitrary")),
        name="diff_attn",
    )(*args, *a_args)


def _mla_kernel(*refs, n_past):
    refs = list(refs)
    sp_ref, ss_ref, acc_ref = refs[-3:]
    y_ref = refs[-4]
    if n_past:
        q_ref, kvs_ref, krs_ref, kvp_ref, krp_ref = refs[:5]
    else:
        q_ref, kvs_ref, krs_ref = refs[:3]
        kvp_ref = krp_ref = None
    qi = pl.program_id(1)
    bq = q_ref.shape[0]
    bias = _chunk_bias(bq, bq)
    scale = (D_NOPE + D_ROPE) ** -0.5 * LOG2E
    for h in range(H_D):
        qn = q_ref[:, h * D_NOPE:(h + 1) * D_NOPE]
        qr = q_ref[:, H_D * D_NOPE + h * LANES:H_D * D_NOPE + h * LANES + D_ROPE]
        ksl = slice(h * (D_NOPE + DV_D), h * (D_NOPE + DV_D) + D_NOPE)
        vsl = slice(h * (D_NOPE + DV_D) + D_NOPE, (h + 1) * (D_NOPE + DV_D))

        def score_fn(src, r0, n):
            kv = kvp_ref if src == "past" else kvs_ref
            kr = krp_ref if src == "past" else krs_ref
            s = _dot_nt(qn, kv[pl.ds(r0, n), ksl]) + _dot_nt(qr, kr[pl.ds(r0, n), :D_ROPE].astype(BF16))
            return s * scale

        def value_fn(src, r0, n):
            return (kvp_ref if src == "past" else kvs_ref)[pl.ds(r0, n), vsl]

        a = _attend(bq, score_fn, value_fn, n_past, bq, qi, bias, sp_ref, ss_ref, acc_ref)
        y_ref[:, h * DV_D:(h + 1) * DV_D] = a.astype(BF16)


def _mla_attn(q16, kv16, P16, past, off, B, T, M, y_prev):
    bq = _q_block(T)
    nq = T // bq
    wq = q16.shape[1]
    wkv = kv16.shape[1]
    kr_blk = ((12 - P16_FIRST) * PROJ_TILE + KV_LORA) // LANES
    in_specs = [
        pl.BlockSpec((bq, wq), lambda b, i: (off // bq + b * nq + i, 0)),
        pl.BlockSpec((T, wkv), lambda b, i: (off // T + b, 0)),
        pl.BlockSpec((T, LANES), lambda b, i: (off // T + b, kr_blk)),
    ]
    args = [q16, kv16, P16]
    n_past = 0
    if past is not None:
        if past[0] == "flat":
            _, poff, n_past = past
            in_specs += [pl.BlockSpec((n_past, wkv), lambda b, i: (poff // n_past + b, 0)),
                         pl.BlockSpec((n_past, LANES), lambda b, i: (poff // n_past + b, kr_blk))]
            args += [kv16, P16]
        else:
            _, pkv, pkr, layer = past
            n_past = pkv.shape[1]
            in_specs += [pl.BlockSpec((None, n_past, wkv), lambda b, i: (b, 0, 0)),
                         pl.BlockSpec((None, None, n_past, D_ROPE), lambda b, i: (layer, b, 0, 0))]
            args += [pkv, pkr]
    width = H_D * DV_D
    a_specs, a_args, aliases = _y_alias(y_prev, len(args))
    return pl.pallas_call(
        functools.partial(_mla_kernel, n_past=n_past),
        grid=(B, nq),
        in_specs=in_specs + a_specs,
        out_specs=pl.BlockSpec((bq, width), lambda b, i: (off // bq + b * nq + i, 3)),
        out_shape=jax.ShapeDtypeStruct((M, N_BRANCH * BRANCH_W), BF16),
        scratch_shapes=_attn_scratch(bq, n_past, nq, bq, DV_D),
        input_output_aliases=aliases,
        compiler_params=_cparams(("parallel", "arbitrary")),
        name="mla_attn",
    )(*args, *a_args)


def _rope_table(pos, period, half, rot_dim, theta):
    lane = np.arange(LANES)
    li = lane % period
    first = li < half
    second = (li >= half) & (li < rot_dim)
    idx = np.where(first, li, np.where(second, li - half, 0))
    freq = jnp.power(jnp.float32(theta), -jnp.arange(half, dtype=F32) / half)[idx]
    ang = pos.astype(F32)[:, None] * freq[None, :]
    cos, sin = jnp.cos(ang), jnp.sin(ang)
    rot = jnp.asarray(first | second)[None, :]
    return [jnp.where(rot, cos, 1.0), jnp.where(jnp.asarray(second)[None, :], sin, 0.0),
            jnp.where(jnp.asarray(first)[None, :], -sin, 0.0)]


def _rope_tables(pos):
    tabs = (_rope_table(pos, LANES, DK_B // 2, DK_B, RET_THETA)
            + _rope_table(pos, DH_C, PARTIAL_ROT // 2, PARTIAL_ROT, ROPE_THETA)
            + _rope_table(pos, LANES, D_ROPE // 2, D_ROPE, ROPE_THETA))
    return jnp.concatenate(tabs, axis=1)


def _uq_layout(w_uq):
    depth = w_uq.shape[0]
    w = w_uq.reshape(depth, Q_LORA, H_D, D_NOPE + D_ROPE)
    nope = w[..., :D_NOPE].reshape(depth, Q_LORA, H_D * D_NOPE)
    rope = jnp.pad(w[..., D_NOPE:], ((0, 0), (0, 0), (0, 0), (0, LANES - D_ROPE))).reshape(depth, Q_LORA, H_D * LANES)
    return jnp.concatenate([nope, rope], axis=2).astype(BF16)


def kernel(x_prompt, x_sample, cache_diff_k, cache_diff_v, cache_mla_ckv, cache_mla_krope, state_hgrn, state_ret, meta_tokens, w_ffn1_in, w_ffn1_out, ln1_g, ln1_b, w_in, lb_logits, hgrn_norm_g, ret_norm_g, diff_lambda_q1, diff_lambda_k1, diff_lambda_q2, diff_lambda_k2, diff_norm_g, mla_q_norm_g, mla_kv_norm_g, w_mla_uq, w_mla_ukv, w_branch, w_merge_gate, w_mix_out, ln2_g, ln2_b, w_ffn2_in, w_ffn2_out, ln3_g, ln3_b):
    depth = w_in.shape[0]
    Bp, S, D = x_prompt.shape
    Bs, Ts, _ = x_sample.shape
    n_cache = cache_diff_k.shape[2]
    past = n_cache - N_META
    assert S % (2 * CHUNK) == 0 and Ts <= CHUNK and past % CHUNK == 0 and (past + Ts - 1) // CHUNK == past // CHUNK
    alpha = (2 * depth) ** 0.25

    off_s = Bp * S
    off_m = off_s + Bs * Ts
    M = off_m + Bp * N_META
    assert off_s % Ts == 0 and off_m % N_META == 0
    x = jnp.concatenate([
        x_prompt.reshape(Bp * S, D), x_sample.reshape(Bs * Ts, D),
        jnp.broadcast_to(meta_tokens[None].astype(x_prompt.dtype), (Bp, N_META, D)).reshape(Bp * N_META, D)], 0)
    pos = jnp.concatenate([
        jnp.tile(N_META + jnp.arange(S, dtype=jnp.int32), Bp),
        jnp.tile(N_META + past + jnp.arange(Ts, dtype=jnp.int32), Bs),
        jnp.tile(jnp.arange(N_META, dtype=jnp.int32), Bp)])
    tab = _rope_tables(pos)

    w1i, w1o = w_ffn1_in.astype(BF16), w_ffn1_out.astype(BF16)
    w2i, w2o = w_ffn2_in.astype(BF16), w_ffn2_out.astype(BF16)
    w_in16 = jnp.pad(w_in, ((0, 0), (0, 0), (0, PROJ_COLS - w_in.shape[2]))).astype(BF16)
    w_uq16 = _uq_layout(w_mla_uq)
    w_ukv16 = w_mla_ukv.astype(BF16)
    w_gate16, w_branch16, w_out16 = w_merge_gate.astype(BF16), w_branch.astype(BF16), w_mix_out.astype(BF16)

    la, lc = _lower_bounds(lb_logits)
    groups = (("meta", off_m, Bp, N_META), ("prompt", 0, Bp, S), ("sample", off_s, Bs, Ts))
    row = lambda a: a.reshape(1, -1).astype(F32)
    out_dt = x_prompt.dtype

    ck_all = cache_diff_k.reshape(depth, Bs, n_cache * H_C, 2 * DH_C)
    cv_all = cache_diff_v.reshape(depth, Bs, n_cache * H_C, DV_C)
    ckv_all = cache_mla_ckv.reshape(depth * Bs * n_cache, KV_LORA)
    sa_all, sb_all = state_hgrn.astype(F32), state_ret.astype(F32)

    def alloc(B, T, tail):
        return jnp.zeros((depth, B, T) + tail, out_dt)

    tails = {"k": (H_C, 2 * DH_C), "v": (H_C, DV_C), "ckv": (KV_LORA,), "kr": (D_ROPE,)}
    p_out = {k: alloc(Bp, N_META + S, t) for k, t in tails.items()}
    s_out = {k: alloc(Bs, Ts, t) for k, t in tails.items()}
    states = {k: [] for k in ("psa", "psb", "ssa", "ssb")}

    for l in range(depth):
        x, x16 = _ffn(x, w1i, w1o, l, row(ln1_g[l]), row(ln1_b[l]), alpha)

        lb2 = jnp.stack([la[l], lc[l]], 0)
        Pa, kc4, vc4, ckr, P16 = _proj(x16, w_in16, l, lb2, row(mla_q_norm_g[l]), row(mla_kv_norm_g[l]), tab)
        q_d = _q_up(P16, w_uq16, l, tab)
        kv_new = _kv_up(P16, ((12 - P16_FIRST) * PROJ_TILE) // KV_LORA, w_ukv16, l)
        kv_past = _kv_up(ckv_all, 0, w_ukv16, l, l * Bs * n_cache, Bs * n_cache).reshape(Bs, n_cache, -1)
        lam_p = jnp.stack([diff_lambda_q1[l], diff_lambda_k1[l], diff_lambda_q2[l], diff_lambda_k2[l]], 0).astype(F32)
        lam_init = 0.8 - 0.6 * math.exp(-0.3 * l)

        y = None
        sa_meta = sb_meta = None
        for name, off, B, T in groups:
            if name == "meta":
                sa0 = sb0 = None
                s_layer = None
                past_c = past_d = None
            elif name == "prompt":
                sa0, sb0, s_layer = sa_meta, sb_meta, None
                past_c = past_d = ("flat", off_m, N_META)
            else:
                sa0, sb0, s_layer = sa_all, sb_all, l
                past_c = ("cache", ck_all, cv_all, l)
                past_d = ("cache", kv_past, cache_mla_krope, l)
            y, sa = _hgrn(Pa, sa0, s_layer, row(hgrn_norm_g[l]), off, B, T, M, y)
            y, sb = _ret(Pa, sb0, s_layer, row(ret_norm_g[l]), off, B, T, M, y)
            y = _diff_attn(P16, past_c, lam_p, row(diff_norm_g[l]), off, B, T, M, lam_init, y)
            y = _mla_attn(q_d, kv_new, P16, past_d, off, B, T, M, y)
            if name == "meta":
                sa_meta, sb_meta = sa, sb
            elif name == "prompt":
                states["psa"].append(sa); states["psb"].append(sb)
            else:
                states["ssa"].append(sa); states["ssb"].append(sb)

        merged = _merge(x16, y, w_gate16, w_branch16, l)
        x, x16 = _mix_out(x, merged, w_out16, l, row(ln2_g[l]), row(ln2_b[l]), alpha)
        x, x16 = _ffn(x, w2i, w2o, l, row(ln3_g[l]), row(ln3_b[l]), alpha)

        pieces = {"k": kc4.reshape(M, H_C, 2 * DH_C), "v": vc4.reshape(M, H_C, DV_C),
                  "ckv": ckr[:, :KV_LORA], "kr": ckr[:, KV_LORA:KV_LORA + D_ROPE]}
        for key, piece in pieces.items():
            tail = tails[key]
            p_out[key] = p_out[key].at[l, :, :N_META].set(piece[off_m:].reshape((Bp, N_META) + tail))
            p_out[key] = p_out[key].at[l, :, N_META:].set(piece[:off_s].reshape((Bp, S) + tail))
            s_out[key] = s_out[key].at[l].set(piece[off_s:off_m].reshape((Bs, Ts) + tail))

    y_prompt = x[:off_s].reshape(Bp, S, D)
    y_sample = x[off_s:off_m].reshape(Bs, Ts, D)
    st = lambda k: jnp.stack(states[k], 0).astype(out_dt)
    return (y_prompt, y_sample, p_out["k"], p_out["v"], p_out["ckv"], p_out["kr"], st("psa"), st("psb"),
            s_out["k"], s_out["v"], s_out["ckv"], s_out["kr"], st("ssa"), st("ssb"))
```

```python
import functools
import math

import numpy as np
import jax
import jax.numpy as jnp
from jax import lax
from jax.experimental import pallas as pl
from jax.experimental.pallas import tpu as pltpu

F32 = jnp.float32
BF16 = jnp.bfloat16

D_MODEL = 2048
CHUNK = 64
N_META = 16
N_BRANCH = 4
BRANCH_W = 512
H_A, DK_A, DV_A = 4, 128, 128
H_B, DK_B, DV_B = 4, 128, 128
RET_THETA = 10000.0
H_C, DH_C, DV_C = 4, 64, 128
PARTIAL_ROT = DH_C // 4
H_D = 4
Q_LORA, KV_LORA = 512, 256
D_NOPE, D_ROPE, DV_D = 128, 64, 128
D_FF = 5632
ROPE_THETA = 500000.0
EPS = 1e-5

LANES = 128
SUBLANES = 8
PROJ_TILE = 512
PROJ_COLS = 13 * PROJ_TILE
P16_FIRST = 8
P16_COLS = PROJ_COLS - P16_FIRST * PROJ_TILE
PA_TILES = 8
VMEM_LIMIT = 56 * 1024 * 1024
NEG_BIG = -1e30
LOG2E = math.log2(math.e)


def _cparams(sem):
    return pltpu.CompilerParams(dimension_semantics=sem, vmem_limit_bytes=VMEM_LIMIT)


def _pick_tile(n, cap, mult=16):
    best = None
    for t in range(mult, min(n, cap) + 1, mult):
        if n % t == 0:
            best = t
    return best if best is not None else n


def _dot(a, b):
    return jnp.dot(a, b, preferred_element_type=F32)


def _dot_nt(a, b):
    return lax.dot_general(a, b, (((1,), (1,)), ((), ())), preferred_element_type=F32)


def _dot_tn(a, b):
    return lax.dot_general(a, b, (((0,), (0,)), ((), ())), preferred_element_type=F32)


def _layer_norm(z, g, b):
    mu = jnp.mean(z, -1, keepdims=True)
    d = z - mu
    var = jnp.mean(d * d, -1, keepdims=True)
    return d * lax.rsqrt(var + EPS) * g + b


def _rms_norm(z, g):
    return z * lax.rsqrt(jnp.mean(z * z, -1, keepdims=True) + EPS) * g


def _silu(a):
    return a * jax.nn.sigmoid(a)


def _rope128(x, c, s1, s2, shift):
    return x * c + pltpu.roll(x, shift, 1) * s1 + pltpu.roll(x, LANES - shift, 1) * s2


def _lower_bound_kernel(logit_ref, la_ref, lc_ref):
    z = logit_ref[...]
    depth = z.shape[0]
    m = z[0:1]
    for l in range(1, depth):
        m = jnp.maximum(m, z[l:l + 1])
    e = jnp.exp(z - m)
    tot = e[0:1]
    for l in range(1, depth):
        tot = tot + e[l:l + 1]
    p = e / tot
    run = jnp.zeros_like(m)
    for l in range(depth):
        la_ref[l:l + 1, :] = jnp.log(run)
        lc_ref[l:l + 1, :] = jnp.log1p(-run)
        run = run + p[l:l + 1]


def _lower_bounds(lb_logits):
    shp = jax.ShapeDtypeStruct(lb_logits.shape, F32)
    return pl.pallas_call(_lower_bound_kernel, out_shape=(shp, shp), name="lower_bounds")(
        lb_logits.astype(F32))


def _ffn_kernel(x_ref, wa_ref, wb_ref, wo_ref, g_ref, b_ref, y32_ref, y16_ref, xs_ref, acc_ref, *, alpha):
    f = pl.program_id(1)

    @pl.when(f == 0)
    def _():
        xs_ref[...] = x_ref[...].astype(BF16)
        acc_ref[...] = jnp.zeros_like(acc_ref)

    x = xs_ref[...]
    a = _dot(x, wa_ref[...])
    b = _dot(x, wb_ref[...])
    h = (_silu(a) * b).astype(BF16)
    acc_ref[...] += _dot(h, wo_ref[...])

    @pl.when(f == pl.num_programs(1) - 1)
    def _():
        y = _layer_norm(alpha * x_ref[...] + 0.5 * acc_ref[...], g_ref[...], b_ref[...])
        y32_ref[...] = y
        y16_ref[...] = y.astype(BF16)


def _ffn(x32, w_in16, w_out16, l, g, b, alpha):
    M, D = x32.shape
    F = w_out16.shape[1]
    tm = _pick_tile(M, 640)
    tf = _pick_tile(F, 512, LANES)
    nf = F // tf
    return pl.pallas_call(
        functools.partial(_ffn_kernel, alpha=alpha),
        grid=(M // tm, nf),
        in_specs=[
            pl.BlockSpec((tm, D), lambda i, f: (i, 0)),
            pl.BlockSpec((None, D, tf), lambda i, f: (l, 0, f)),
            pl.BlockSpec((None, D, tf), lambda i, f: (l, 0, nf + f)),
            pl.BlockSpec((None, tf, D), lambda i, f: (l, f, 0)),
            pl.BlockSpec((1, D), lambda i, f: (0, 0)),
            pl.BlockSpec((1, D), lambda i, f: (0, 0)),
        ],
        out_specs=(pl.BlockSpec((tm, D), lambda i, f: (i, 0)),
                   pl.BlockSpec((tm, D), lambda i, f: (i, 0))),
        out_shape=(jax.ShapeDtypeStruct((M, D), F32), jax.ShapeDtypeStruct((M, D), BF16)),
        scratch_shapes=[pltpu.VMEM((tm, D), BF16), pltpu.VMEM((tm, D), F32)],
        compiler_params=_cparams(("parallel", "arbitrary")),
        name="ffn",
    )(x32, w_in16, w_in16, w_out16, g, b)


def _proj_kernel(x_ref, w_ref, lb_ref, qg_ref, kvg_ref, tab_ref, pa_ref, kc_ref, vc_ref, ckr_ref, o16_ref):
    j = pl.program_id(1)
    tm = x_ref.shape[0]

    def acc():
        return _dot(x_ref[...], w_ref[...])

    def tab(k):
        return tab_ref[:, k * LANES:(k + 1) * LANES]

    def rope_heads(t0, shift):
        c, s1, s2 = tab(t0), tab(t0 + 1), tab(t0 + 2)
        a = acc()
        return [_rope128(a[:, h * LANES:(h + 1) * LANES], c, s1, s2, shift) for h in range(PROJ_TILE // LANES)]

    @pl.when((j == 0) | (j == 3) | (j == 7))
    def _():
        pa_ref[...] = _silu(acc())

    @pl.when(j == 1)
    def _():
        z = acc()
        log_sig = -(jnp.maximum(-z, 0.0) + jnp.log1p(jnp.exp(-jnp.abs(z))))
        a = lb_ref[0:1, :]
        c = lb_ref[1:2, :] + log_sig
        pa_ref[...] = jnp.maximum(a, c) + jnp.log1p(jnp.exp(-jnp.abs(a - c)))

    @pl.when((j == 2) | (j == 6))
    def _():
        pa_ref[...] = acc()

    @pl.when(j == 4)
    def _():
        for h, r in enumerate(rope_heads(0, DK_B // 2)):
            pa_ref[:, h * LANES:(h + 1) * LANES] = r

    @pl.when(j == 5)
    def _():
        for h, r in enumerate(rope_heads(0, DK_B // 2)):
            pa_ref[:, h * LANES:(h + 1) * LANES] = r * DK_B ** -0.5

    @pl.when(j == 8)
    def _():
        for h, r in enumerate(rope_heads(3, PARTIAL_ROT // 2)):
            o16_ref[:, h * LANES:(h + 1) * LANES] = r.astype(BF16)

    @pl.when(j == 9)
    def _():
        for h, r in enumerate(rope_heads(3, PARTIAL_ROT // 2)):
            kc_ref[pl.ds(h, tm, stride=H_C), :] = r
            o16_ref[:, h * LANES:(h + 1) * LANES] = r.astype(BF16)

    @pl.when(j == 10)
    def _():
        a = acc()
        for h in range(H_C):
            vc_ref[pl.ds(h, tm, stride=H_C), :] = a[:, h * LANES:(h + 1) * LANES]
        o16_ref[...] = a.astype(BF16)

    @pl.when(j == 11)
    def _():
        o16_ref[...] = _rms_norm(acc(), qg_ref[...]).astype(BF16)

    @pl.when(j == 12)
    def _():
        a = acc()
        ckv = _rms_norm(a[:, :KV_LORA], kvg_ref[...])
        kr = _rope128(a[:, KV_LORA:KV_LORA + LANES], tab(6), tab(7), tab(8), D_ROPE // 2)
        zeros = jnp.zeros((tm, PROJ_TILE - KV_LORA - LANES), F32)
        ckr_ref[:, :KV_LORA] = ckv
        ckr_ref[:, KV_LORA:KV_LORA + LANES] = kr
        ckr_ref[:, KV_LORA + LANES:] = zeros
        o16_ref[:, :KV_LORA] = ckv.astype(BF16)
        o16_ref[:, KV_LORA:KV_LORA + LANES] = kr.astype(BF16)
        o16_ref[:, KV_LORA + LANES:] = zeros.astype(BF16)


def _proj(x16, w16, l, lb2, qg, kvg, tab):
    M, D = x16.shape
    tm = _pick_tile(M, 1024)
    nj = PROJ_COLS // PROJ_TILE
    once = lambda i, j: (i, 0)
    return pl.pallas_call(
        _proj_kernel,
        grid=(M // tm, nj),
        in_specs=[
            pl.BlockSpec((tm, D), once),
            pl.BlockSpec((None, D, PROJ_TILE), lambda i, j: (l, 0, j)),
            pl.BlockSpec((2, PROJ_TILE), lambda i, j: (0, 0)),
            pl.BlockSpec((1, Q_LORA), lambda i, j: (0, 0)),
            pl.BlockSpec((1, KV_LORA), lambda i, j: (0, 0)),
            pl.BlockSpec((tm, 9 * LANES), once),
        ],
        out_specs=(pl.BlockSpec((tm, PROJ_TILE), lambda i, j: (i, jnp.minimum(j, PA_TILES - 1))),
                   pl.BlockSpec((tm * H_C, LANES), once),
                   pl.BlockSpec((tm * H_C, LANES), once),
                   pl.BlockSpec((tm, PROJ_TILE), once),
                   pl.BlockSpec((tm, PROJ_TILE), lambda i, j: (i, jnp.maximum(j - P16_FIRST, 0)))),
        out_shape=(jax.ShapeDtypeStruct((M, PA_TILES * PROJ_TILE), F32),
                   jax.ShapeDtypeStruct((M * H_C, LANES), F32),
                   jax.ShapeDtypeStruct((M * H_C, LANES), F32),
                   jax.ShapeDtypeStruct((M, PROJ_TILE), F32),
                   jax.ShapeDtypeStruct((M, P16_COLS), BF16)),
        compiler_params=_cparams(("parallel", "arbitrary")),
        name="proj",
    )(x16, w16, lb2, qg, kvg, tab)


def _q_up_kernel(x_ref, w_ref, tab_ref, o_ref):
    acc = _dot(x_ref[...], w_ref[...])
    nope = H_D * D_NOPE
    o_ref[:, :nope] = acc[:, :nope].astype(BF16)
    c, s1, s2 = (tab_ref[:, k * LANES:(k + 1) * LANES] for k in range(3))
    for h in range(H_D):
        lo = nope + h * LANES
        o_ref[:, lo:lo + LANES] = _rope128(acc[:, lo:lo + LANES], c, s1, s2, D_ROPE // 2).astype(BF16)


def _q_up(P16, w16, l, tab):
    M = P16.shape[0]
    N = w16.shape[2]
    tm = _pick_tile(M, 640)
    return pl.pallas_call(
        _q_up_kernel,
        grid=(M // tm,),
        in_specs=[
            pl.BlockSpec((tm, Q_LORA), lambda i: (i, 11 - P16_FIRST)),
            pl.BlockSpec((None, Q_LORA, N), lambda i: (l, 0, 0)),
            pl.BlockSpec((tm, 3 * LANES), lambda i: (i, 2)),
        ],
        out_specs=pl.BlockSpec((tm, N), lambda i: (i, 0)),
        out_shape=jax.ShapeDtypeStruct((M, N), BF16),
        compiler_params=_cparams(("parallel",)),
        name="q_up",
    )(P16, w16, tab)


def _kv_up_kernel(x_ref, w_ref, o_ref):
    o_ref[...] = _dot(x_ref[...].astype(BF16), w_ref[...]).astype(BF16)


def _kv_up(x, col_block, w16, l, row0=0, n_rows=None):
    M = x.shape[0] if n_rows is None else n_rows
    N = w16.shape[2]
    tm = _pick_tile(math.gcd(M, row0) if row0 else M, 1024)
    blk0 = row0 // tm
    return pl.pallas_call(
        _kv_up_kernel,
        grid=(M // tm,),
        in_specs=[
            pl.BlockSpec((tm, KV_LORA), lambda i: (blk0 + i, col_block)),
            pl.BlockSpec((None, KV_LORA, N), lambda i: (l, 0, 0)),
        ],
        out_specs=pl.BlockSpec((tm, N), lambda i: (i, 0)),
        out_shape=jax.ShapeDtypeStruct((M, N), BF16),
        compiler_params=_cparams(("parallel",)),
        name="kv_up",
    )(x, w16)


def _merge_kernel(h_ref, y_ref, wg_ref, wb_ref, o_ref, acc_ref):
    n = pl.program_id(2)

    @pl.when(n == 0)
    def _():
        acc_ref[...] = jnp.zeros_like(acc_ref)

    gate = jax.nn.sigmoid(_dot(h_ref[...], wg_ref[...]))
    acc_ref[...] += gate * _dot(y_ref[...], wb_ref[...])

    @pl.when(n == pl.num_programs(2) - 1)
    def _():
        o_ref[...] = acc_ref[...].astype(BF16)


def _merge(h16, y, wg16, wb16, l):
    M, D = h16.shape
    W = BRANCH_W
    tm = _pick_tile(M, 640)
    tn = 1024
    return pl.pallas_call(
        _merge_kernel,
        grid=(M // tm, D // tn, N_BRANCH),
        in_specs=[
            pl.BlockSpec((tm, D), lambda i, j, n: (i, 0)),
            pl.BlockSpec((tm, W), lambda i, j, n: (i, n)),
            pl.BlockSpec((None, None, D, tn), lambda i, j, n: (l, n, 0, j)),
            pl.BlockSpec((None, None, W, tn), lambda i, j, n: (l, n, 0, j)),
        ],
        out_specs=pl.BlockSpec((tm, tn), lambda i, j, n: (i, j)),
        out_shape=jax.ShapeDtypeStruct((M, D), BF16),
        scratch_shapes=[pltpu.VMEM((tm, tn), F32)],
        compiler_params=_cparams(("parallel", "arbitrary", "arbitrary")),
        name="merge",
    )(h16, y, wg16, wb16)


def _mix_out_kernel(x_ref, m_ref, w_ref, g_ref, b_ref, y32_ref, y16_ref, *, alpha):
    y = _layer_norm(alpha * x_ref[...] + _dot(m_ref[...], w_ref[...]), g_ref[...], b_ref[...])
    y32_ref[...] = y
    y16_ref[...] = y.astype(BF16)


def _mix_out(x32, merged16, w16, l, g, b, alpha):
    M, D = x32.shape
    tm = _pick_tile(M, 320)
    return pl.pallas_call(
        functools.partial(_mix_out_kernel, alpha=alpha),
        grid=(M // tm,),
        in_specs=[
            pl.BlockSpec((tm, D), lambda i: (i, 0)),
            pl.BlockSpec((tm, D), lambda i: (i, 0)),
            pl.BlockSpec((None, D, D), lambda i: (l, 0, 0)),
            pl.BlockSpec((1, D), lambda i: (0, 0)),
            pl.BlockSpec((1, D), lambda i: (0, 0)),
        ],
        out_specs=(pl.BlockSpec((tm, D), lambda i: (i, 0)),
                   pl.BlockSpec((tm, D), lambda i: (i, 0))),
        out_shape=(jax.ShapeDtypeStruct((M, D), F32), jax.ShapeDtypeStruct((M, D), BF16)),
        compiler_params=_cparams(("parallel",)),
        name="mix_out",
    )(x32, merged16, w16, g, b)


def _y_alias(y_prev, n_inputs):
    if y_prev is None:
        return [], [], {}
    return [pl.BlockSpec(memory_space=pl.ANY)], [y_prev], {n_inputs: 0}


def _state_spec(s0, layer, H, DK, DV):
    if layer is None:
        return pl.BlockSpec((None, H, DK, DV), lambda b, c: (b, 0, 0, 0))
    return pl.BlockSpec((None, None, H, DK, DV), lambda b, c: (layer, b, 0, 0, 0))


def _hgrn_kernel(*refs, C, has_s0):
    q_ref, lf_ref, v_ref, gate_ref = refs[:4]
    s0_ref = refs[4] if has_s0 else None
    g_ref = refs[4 + has_s0]
    y_ref, s_out_ref, st_ref = refs[-3:]
    step = pl.program_id(1)
    n_sub = q_ref.shape[0] // C
    nv = C // SUBLANES

    @pl.when(step == 0)
    def _():
        for h in range(H_A):
            st_ref[h] = s0_ref[h].T if has_s0 else jnp.zeros((DV_A, DK_A), F32)

    row8 = lax.broadcasted_iota(jnp.int32, (SUBLANES, LANES), 0)
    lane8 = lax.broadcasted_iota(jnp.int32, (SUBLANES, C), 1)
    tril = (lax.broadcasted_iota(jnp.int32, (C, C), 1) <= lax.broadcasted_iota(jnp.int32, (C, C), 0)).astype(F32)

    def chunk(ci, carry):
        rs = pl.ds(pl.multiple_of(ci * C, C), C)
        heads = range(H_A)
        sls = [slice(h * LANES, (h + 1) * LANES) for h in heads]
        b_all = jnp.dot(tril, lf_ref[rs, :], preferred_element_type=F32, precision=lax.Precision.HIGHEST)
        q = [q_ref[rs, sl] for sl in sls]
        v16 = [v_ref[rs, sl].astype(BF16) for sl in sls]
        k = [1.0 - jnp.exp(lf_ref[rs, sl]) for sl in sls]
        b = [b_all[:, sl] for sl in sls]
        st = [st_ref[h] for h in heads]
        o = [_dot_nt((q[h] * jnp.exp(b[h])).astype(BF16), st[h].astype(BF16)) for h in heads]
        res = []
        for h in heads:
            b2 = b[h] * LOG2E
            pieces = []
            for s in range(C):
                r0 = (s // SUBLANES) * SUBLANES
                d = b2[r0:] - b2[s:s + 1]
                head = jnp.where(row8 >= s % SUBLANES, d[:SUBLANES], -jnp.inf)
                d = head if C - r0 == SUBLANES else jnp.concatenate([head, d[SUBLANES:]], 0)
                pieces.append(q[h][r0:] * jnp.exp2(d))
            res.append(_dot_nt(jnp.concatenate(pieces, 0).astype(BF16), k[h].astype(BF16)))
        for h in heads:
            a_parts = [jnp.zeros((SUBLANES, C), F32) for _ in range(nv)]
            off = 0
            for s in range(C):
                for i in range(s // SUBLANES, nv):
                    a_parts[i] = a_parts[i] + jnp.where(lane8 == s, res[h][off:off + SUBLANES], 0.0)
                    off += SUBLANES
            o[h] = o[h] + _dot(jnp.concatenate(a_parts, 0).astype(BF16), v16[h])
        for h in heads:
            b_last = b[h][C - 1:C, :]
            kd = (k[h] * jnp.exp(b_last - b[h])).astype(BF16)
            st_ref[h] = st[h] * jnp.exp(b_last) + _dot_tn(v16[h], kd)
            y_ref[rs, sls[h]] = (_rms_norm(o[h], g_ref[...]) * gate_ref[rs, sls[h]]).astype(BF16)
        return carry

    lax.fori_loop(0, n_sub, chunk, 0)

    @pl.when(step == pl.num_programs(1) - 1)
    def _():
        for h in range(H_A):
            s_out_ref[h] = st_ref[h].T


def _hgrn(Pa, s0, s0_layer, g, off, B, T, M, y_prev):
    C = min(T, 64)
    rows_blk = min(T, 256)
    nb = T // rows_blk
    base = off // rows_blk
    width = H_A * DV_A

    def rows(colblk):
        return pl.BlockSpec((rows_blk, width), lambda b, c: (base + b * nb + c, colblk))

    s_specs, s_args = ([], []) if s0 is None else ([_state_spec(s0, s0_layer, H_A, DK_A, DV_A)], [s0])
    a_specs, a_args, aliases = _y_alias(y_prev, 5 + len(s_args))
    return pl.pallas_call(
        functools.partial(_hgrn_kernel, C=C, has_s0=bool(s_args)),
        grid=(B, nb),
        in_specs=[rows(0), rows(1), rows(2), rows(3)] + s_specs + [pl.BlockSpec((1, DV_A), lambda b, c: (0, 0))] + a_specs,
        out_specs=(pl.BlockSpec((rows_blk, width), lambda b, c: (base + b * nb + c, 0)),
                   pl.BlockSpec((None, H_A, DK_A, DV_A), lambda b, c: (b, 0, 0, 0))),
        out_shape=(jax.ShapeDtypeStruct((M, N_BRANCH * BRANCH_W), BF16),
                   jax.ShapeDtypeStruct((B, H_A, DK_A, DV_A), F32)),
        scratch_shapes=[pltpu.VMEM((H_A, DV_A, DK_A), F32)],
        input_output_aliases=aliases,
        compiler_params=_cparams(("parallel", "arbitrary")),
        name="hgrn",
    )(Pa, Pa, Pa, Pa, *s_args, g, *a_args)


def _ret_kernel(*refs, has_s0):
    q_ref, k_ref, v_ref, gate_ref = refs[:4]
    s0_ref = refs[4] if has_s0 else None
    g_ref = refs[4 + has_s0]
    y_ref, s_out_ref, s_ref = refs[-3:]
    c = pl.program_id(1)
    C = q_ref.shape[0]

    @pl.when(c == 0)
    def _():
        s_ref[...] = s0_ref[...] if has_s0 else jnp.zeros_like(s_ref)

    ti = lax.broadcasted_iota(jnp.int32, (C, C), 0)
    si = lax.broadcasted_iota(jnp.int32, (C, C), 1)
    dist = (ti - si).astype(F32)
    t1 = (lax.broadcasted_iota(jnp.int32, (C, 1), 0) + 1).astype(F32)

    for h in range(H_B):
        sl = slice(h * LANES, (h + 1) * LANES)
        log_gamma = math.log1p(-(2.0 ** (-5.0 - h)))
        q = q_ref[:, sl]
        k = k_ref[:, sl]
        v16 = v_ref[:, sl].astype(BF16)
        decay = jnp.exp(jnp.where(ti >= si, dist * log_gamma, -jnp.inf))
        a = _dot_nt(q.astype(BF16), k.astype(BF16)) * decay
        s = s_ref[h]
        o = _dot(a.astype(BF16), v16) + _dot((q * jnp.exp(t1 * log_gamma)).astype(BF16), s.astype(BF16))
        kd = (k * jnp.exp((C - t1) * log_gamma)).astype(BF16)
        s_ref[h] = math.exp(C * log_gamma) * s + _dot_tn(kd, v16)
        mu = jnp.mean(o, -1, keepdims=True)
        d = o - mu
        var = jnp.mean(d * d, -1, keepdims=True)
        y = d * lax.rsqrt(var + EPS) * g_ref[...] * gate_ref[:, sl]
        y_ref[:, sl] = y.astype(BF16)

    @pl.when(c == pl.num_programs(1) - 1)
    def _():
        s_out_ref[...] = s_ref[...]


def _ret(Pa, s0, s0_layer, g, off, B, T, M, y_prev):
    C = min(T, 128)
    nc = T // C
    base = off // C
    width = H_B * DV_B

    def rows(colblk):
        return pl.BlockSpec((C, width), lambda b, c: (base + b * nc + c, colblk))

    s_specs, s_args = ([], []) if s0 is None else ([_state_spec(s0, s0_layer, H_B, DK_B, DV_B)], [s0])
    a_specs, a_args, aliases = _y_alias(y_prev, 5 + len(s_args))
    return pl.pallas_call(
        functools.partial(_ret_kernel, has_s0=bool(s_args)),
        grid=(B, nc),
        in_specs=[rows(4), rows(5), rows(6), rows(7)] + s_specs + [pl.BlockSpec((1, DV_B), lambda b, c: (0, 0))] + a_specs,
        out_specs=(pl.BlockSpec((C, width), lambda b, c: (base + b * nc + c, 1)),
                   pl.BlockSpec((None, H_B, DK_B, DV_B), lambda b, c: (b, 0, 0, 0))),
        out_shape=(jax.ShapeDtypeStruct((M, N_BRANCH * BRANCH_W), BF16),
                   jax.ShapeDtypeStruct((B, H_B, DK_B, DV_B), F32)),
        scratch_shapes=[pltpu.VMEM((H_B, DK_B, DV_B), F32)],
        input_output_aliases=aliases,
        compiler_params=_cparams(("parallel", "arbitrary")),
        name="retention",
    )(Pa, Pa, Pa, Pa, *s_args, g, *a_args)


def _past_blocks(n, cap=512):
    out, r = [], 0
    while r < n:
        w = min(cap, n - r)
        out.append((r, w))
        r += w
    return out


def _colmax(mrun, s):
    n = s.shape[1]
    if n % LANES:
        return jnp.maximum(mrun, jnp.max(s, -1, keepdims=True))
    for gi in range(n // LANES):
        mrun = jnp.maximum(mrun, s[:, gi * LANES:(gi + 1) * LANES])
    return mrun


def _attend(n_heads, score_fn, value_fn, n_past, bq, qi, bias, sp_ref, ss_ref, m_ref, acc_ref):
    blocks = _past_blocks(n_past)
    heads = range(n_heads)
    m_ref[...] = jnp.full(m_ref.shape, NEG_BIG, F32)

    def keep(h, dst, idx, s):
        dst[idx] = s
        m_ref[h] = _colmax(m_ref[h], s)

    for r0, n in blocks:
        for h in heads:
            keep(h, sp_ref, (h, slice(None), slice(r0, r0 + n)), score_fn(h, "past", r0, n))

    def scores(j, carry):
        r0 = pl.multiple_of(j * bq, bq)
        for h in heads:
            keep(h, ss_ref, (h, j), score_fn(h, "self", r0, bq))
        return carry

    lax.fori_loop(0, qi, scores, 0)
    r_diag = pl.multiple_of(qi * bq, bq)
    for h in heads:
        s = score_fn(h, "self", r_diag, bq)
        keep(h, ss_ref, (h, qi), s if bias is None else s + bias)
    m = [jnp.max(m_ref[h], -1, keepdims=True) for h in heads]

    acc_ref[...] = jnp.zeros_like(acc_ref)

    def accumulate(h, s, v16):
        p = jnp.exp2(s - m[h]).astype(BF16)
        acc_ref[h] += _dot(p, jnp.concatenate([v16, jnp.ones_like(v16)], 1))

    for r0, n in blocks:
        for h in heads:
            accumulate(h, sp_ref[h, :, r0:r0 + n], value_fn(h, "past", r0, n))

    def weighted(j, carry):
        r0 = pl.multiple_of(j * bq, bq)
        for h in heads:
            accumulate(h, ss_ref[h, j], value_fn(h, "self", r0, bq))
        return carry

    lax.fori_loop(0, qi, weighted, 0)
    for h in heads:
        accumulate(h, ss_ref[h, qi], value_fn(h, "self", r_diag, bq))
    dv = acc_ref.shape[2] // 2
    return [acc_ref[h, :, :dv] / acc_ref[h, :, dv:] for h in heads]


def _chunk_bias(rows, bq):
    if bq <= CHUNK:
        return None
    qc = (lax.broadcasted_iota(jnp.int32, (rows, bq), 0) % bq) // CHUNK
    kc = lax.broadcasted_iota(jnp.int32, (rows, bq), 1) // CHUNK
    return jnp.where(kc <= qc, 0.0, -jnp.inf).astype(F32)


def _attn_scratch(n_heads, rows, n_past, nq, bq, dv):
    return [pltpu.VMEM((n_heads, rows, max(LANES, -(-n_past // LANES) * LANES)), F32),
            pltpu.VMEM((n_heads, nq, rows, bq), F32),
            pltpu.VMEM((n_heads, rows, LANES), F32),
            pltpu.VMEM((n_heads, rows, 2 * dv), F32)]


def _diff_kernel(*refs, past_mode, n_past, lam_init):
    refs = list(refs)
    sp_ref, ss_ref, m_ref, acc_ref = refs[-4:]
    y_ref = refs[-5]
    if past_mode:
        q_ref, ks_ref, vs_ref, kp_ref, vp_ref, lam_ref, g_ref = refs[:7]
    else:
        q_ref, ks_ref, vs_ref, lam_ref, g_ref = refs[:5]
        kp_ref = vp_ref = None
    qi = pl.program_id(1)
    bq = q_ref.shape[0]
    lam_p = lam_ref[...]
    lam = (jnp.exp(jnp.sum(lam_p[0:1] * lam_p[1:2], -1, keepdims=True))
           - jnp.exp(jnp.sum(lam_p[2:3] * lam_p[3:4], -1, keepdims=True)) + lam_init)
    lane = lax.broadcasted_iota(jnp.int32, (bq, LANES), 1)
    bias = _chunk_bias(2 * bq, bq)
    scale = DH_C ** -0.5 * LOG2E
    sls = [slice(h * LANES, (h + 1) * LANES) for h in range(H_C)]
    q2 = []
    for sl in sls:
        qh = q_ref[:, sl]
        zero = jnp.zeros_like(qh)
        q2.append(jnp.concatenate([jnp.where(lane < DH_C, qh, zero), jnp.where(lane >= DH_C, qh, zero)], 0))

    def rows_of(ref, h, src, r0, n):
        if src == "past" and past_mode == "cache":
            return ref[pl.ds(H_C * r0 + h, n, stride=H_C), :]
        return ref[pl.ds(r0, n), sls[h]]

    def score_fn(h, src, r0, n):
        k = rows_of(kp_ref if src == "past" else ks_ref, h, src, r0, n)
        return _dot_nt(q2[h], k.astype(BF16)) * scale

    def value_fn(h, src, r0, n):
        return rows_of(vp_ref if src == "past" else vs_ref, h, src, r0, n).astype(BF16)

    a = _attend(H_C, score_fn, value_fn, n_past, bq, qi, bias, sp_ref, ss_ref, m_ref, acc_ref)
    for h, sl in enumerate(sls):
        d = a[h][:bq] - lam * a[h][bq:]
        y_ref[:, sl] = (_rms_norm(d, g_ref[...]) * (1.0 - lam_init)).astype(BF16)


def _q_block(T):
    for bq in (256, 128):
        if T % bq == 0:
            return bq
    return T


def _diff_attn(P16, past, lam_p, g, off, B, T, M, lam_init, y_prev):
    bq = _q_block(T)
    nq = T // bq
    width = H_C * DV_C
    in_specs = [
        pl.BlockSpec((bq, width), lambda b, i: (off // bq + b * nq + i, 0)),
        pl.BlockSpec((T, width), lambda b, i: (off // T + b, 1)),
        pl.BlockSpec((T, width), lambda b, i: (off // T + b, 2)),
    ]
    args = [P16, P16, P16]
    n_past, past_mode = 0, None
    if past is not None:
        past_mode = past[0]
        if past_mode == "flat":
            _, poff, n_past = past
            in_specs += [pl.BlockSpec((n_past, width), lambda b, i: (poff // n_past + b, 1)),
                         pl.BlockSpec((n_past, width), lambda b, i: (poff // n_past + b, 2))]
            args += [P16, P16]
        else:
            _, pk, pv, layer = past
            n_past = pk.shape[2] // H_C
            spec = pl.BlockSpec((None, None, n_past * H_C, LANES), lambda b, i: (layer, b, 0, 0))
            in_specs += [spec, spec]
            args += [pk, pv]
    in_specs += [pl.BlockSpec((4, DH_C), lambda b, i: (0, 0)), pl.BlockSpec((1, DV_C), lambda b, i: (0, 0))]
    args += [lam_p, g]
    a_specs, a_args, aliases = _y_alias(y_prev, len(args))
    return pl.pallas_call(
        functools.partial(_diff_kernel, past_mode=past_mode, n_past=n_past, lam_init=lam_init),
        grid=(B, nq),
        in_specs=in_specs + a_specs,
        out_specs=pl.BlockSpec((bq, width), lambda b, i: (off // bq + b * nq + i, 2)),
        out_shape=jax.ShapeDtypeStruct((M, N_BRANCH * BRANCH_W), BF16),
        scratch_shapes=_attn_scratch(H_C, 2 * bq, n_past, nq, bq, DV_C),
        input_output_aliases=aliases,
        compiler_params=_cparams(("parallel", "arbitrary")),
        name="diff_attn",
    )(*args, *a_args)


def _mla_kernel(*refs, n_past):
    refs = list(refs)
    sp_ref, ss_ref, m_ref, acc_ref = refs[-4:]
    y_ref = refs[-5]
    if n_past:
        q_ref, kvs_ref, krs_ref, kvp_ref, krp_ref = refs[:5]
    else:
        q_ref, kvs_ref, krs_ref = refs[:3]
        kvp_ref = krp_ref = None
    qi = pl.program_id(1)
    bq = q_ref.shape[0]
    bias = _chunk_bias(bq, bq)
    scale = (D_NOPE + D_ROPE) ** -0.5 * LOG2E
    qn = [q_ref[:, h * D_NOPE:(h + 1) * D_NOPE] for h in range(H_D)]
    qr = [q_ref[:, H_D * D_NOPE + h * LANES:H_D * D_NOPE + h * LANES + D_ROPE] for h in range(H_D)]

    def score_fn(h, src, r0, n):
        kv = kvp_ref if src == "past" else kvs_ref
        kr = krp_ref if src == "past" else krs_ref
        ksl = slice(h * (D_NOPE + DV_D), h * (D_NOPE + DV_D) + D_NOPE)
        s = _dot_nt(qn[h], kv[pl.ds(r0, n), ksl]) + _dot_nt(qr[h], kr[pl.ds(r0, n), :D_ROPE].astype(BF16))
        return s * scale

    def value_fn(h, src, r0, n):
        vsl = slice(h * (D_NOPE + DV_D) + D_NOPE, (h + 1) * (D_NOPE + DV_D))
        return (kvp_ref if src == "past" else kvs_ref)[pl.ds(r0, n), vsl]

    a = _attend(H_D, score_fn, value_fn, n_past, bq, qi, bias, sp_ref, ss_ref, m_ref, acc_ref)
    for h in range(H_D):
        y_ref[:, h * DV_D:(h + 1) * DV_D] = a[h].astype(BF16)


def _mla_attn(q16, kv16, P16, past, off, B, T, M, y_prev):
    bq = _q_block(T)
    nq = T // bq
    wq = q16.shape[1]
    wkv = kv16.shape[1]
    kr_blk = ((12 - P16_FIRST) * PROJ_TILE + KV_LORA) // LANES
    in_specs = [
        pl.BlockSpec((bq, wq), lambda b, i: (off // bq + b * nq + i, 0)),
        pl.BlockSpec((T, wkv), lambda b, i: (off // T + b, 0)),
        pl.BlockSpec((T, LANES), lambda b, i: (off // T + b, kr_blk)),
    ]
    args = [q16, kv16, P16]
    n_past = 0
    if past is not None:
        if past[0] == "flat":
            _, poff, n_past = past
            in_specs += [pl.BlockSpec((n_past, wkv), lambda b, i: (poff // n_past + b, 0)),
                         pl.BlockSpec((n_past, LANES), lambda b, i: (poff // n_past + b, kr_blk))]
            args += [kv16, P16]
        else:
            _, pkv, pkr, layer = past
            n_past = pkv.shape[1]
            in_specs += [pl.BlockSpec((None, n_past, wkv), lambda b, i: (b, 0, 0)),
                         pl.BlockSpec((None, None, n_past, D_ROPE), lambda b, i: (layer, b, 0, 0))]
            args += [pkv, pkr]
    width = H_D * DV_D
    a_specs, a_args, aliases = _y_alias(y_prev, len(args))
    return pl.pallas_call(
        functools.partial(_mla_kernel, n_past=n_past),
        grid=(B, nq),
        in_specs=in_specs + a_specs,
        out_specs=pl.BlockSpec((bq, width), lambda b, i: (off // bq + b * nq + i, 3)),
        out_shape=jax.ShapeDtypeStruct((M, N_BRANCH * BRANCH_W), BF16),
        scratch_shapes=_attn_scratch(H_D, bq, n_past, nq, bq, DV_D),
        input_output_aliases=aliases,
        compiler_params=_cparams(("parallel", "arbitrary")),
        name="mla_attn",
    )(*args, *a_args)


def _rope_table(pos, period, half, rot_dim, theta):
    lane = np.arange(LANES)
    li = lane % period
    first = li < half
    second = (li >= half) & (li < rot_dim)
    idx = np.where(first, li, np.where(second, li - half, 0))
    freq = jnp.power(jnp.float32(theta), -jnp.arange(half, dtype=F32) / half)[idx]
    ang = pos.astype(F32)[:, None] * freq[None, :]
    cos, sin = jnp.cos(ang), jnp.sin(ang)
    rot = jnp.asarray(first | second)[None, :]
    return [jnp.where(rot, cos, 1.0), jnp.where(jnp.asarray(second)[None, :], sin, 0.0),
            jnp.where(jnp.asarray(first)[None, :], -sin, 0.0)]


def _rope_tables(pos):
    tabs = (_rope_table(pos, LANES, DK_B // 2, DK_B, RET_THETA)
            + _rope_table(pos, DH_C, PARTIAL_ROT // 2, PARTIAL_ROT, ROPE_THETA)
            + _rope_table(pos, LANES, D_ROPE // 2, D_ROPE, ROPE_THETA))
    return jnp.concatenate(tabs, axis=1)


def _uq_layout(w_uq):
    depth = w_uq.shape[0]
    w = w_uq.reshape(depth, Q_LORA, H_D, D_NOPE + D_ROPE)
    nope = w[..., :D_NOPE].reshape(depth, Q_LORA, H_D * D_NOPE)
    rope = jnp.pad(w[..., D_NOPE:], ((0, 0), (0, 0), (0, 0), (0, LANES - D_ROPE))).reshape(depth, Q_LORA, H_D * LANES)
    return jnp.concatenate([nope, rope], axis=2).astype(BF16)


def kernel(x_prompt, x_sample, cache_diff_k, cache_diff_v, cache_mla_ckv, cache_mla_krope, state_hgrn, state_ret, meta_tokens, w_ffn1_in, w_ffn1_out, ln1_g, ln1_b, w_in, lb_logits, hgrn_norm_g, ret_norm_g, diff_lambda_q1, diff_lambda_k1, diff_lambda_q2, diff_lambda_k2, diff_norm_g, mla_q_norm_g, mla_kv_norm_g, w_mla_uq, w_mla_ukv, w_branch, w_merge_gate, w_mix_out, ln2_g, ln2_b, w_ffn2_in, w_ffn2_out, ln3_g, ln3_b):
    depth = w_in.shape[0]
    Bp, S, D = x_prompt.shape
    Bs, Ts, _ = x_sample.shape
    n_cache = cache_diff_k.shape[2]
    past = n_cache - N_META
    assert S % (2 * CHUNK) == 0 and Ts <= CHUNK and past % CHUNK == 0 and (past + Ts - 1) // CHUNK == past // CHUNK
    alpha = (2 * depth) ** 0.25

    off_s = Bp * S
    off_m = off_s + Bs * Ts
    M = off_m + Bp * N_META
    assert off_s % Ts == 0 and off_m % N_META == 0
    x = jnp.concatenate([
        x_prompt.reshape(Bp * S, D), x_sample.reshape(Bs * Ts, D),
        jnp.broadcast_to(meta_tokens[None].astype(x_prompt.dtype), (Bp, N_META, D)).reshape(Bp * N_META, D)], 0)
    pos = jnp.concatenate([
        jnp.tile(N_META + jnp.arange(S, dtype=jnp.int32), Bp),
        jnp.tile(N_META + past + jnp.arange(Ts, dtype=jnp.int32), Bs),
        jnp.tile(jnp.arange(N_META, dtype=jnp.int32), Bp)])
    tab = _rope_tables(pos)

    w1i, w1o = w_ffn1_in.astype(BF16), w_ffn1_out.astype(BF16)
    w2i, w2o = w_ffn2_in.astype(BF16), w_ffn2_out.astype(BF16)
    w_in16 = jnp.pad(w_in, ((0, 0), (0, 0), (0, PROJ_COLS - w_in.shape[2]))).astype(BF16)
    w_uq16 = _uq_layout(w_mla_uq)
    w_ukv16 = w_mla_ukv.astype(BF16)
    w_gate16, w_branch16, w_out16 = w_merge_gate.astype(BF16), w_branch.astype(BF16), w_mix_out.astype(BF16)

    la, lc = _lower_bounds(lb_logits)
    groups = (("meta", off_m, Bp, N_META), ("prompt", 0, Bp, S), ("sample", off_s, Bs, Ts))
    row = lambda a: a.reshape(1, -1).astype(F32)
    out_dt = x_prompt.dtype

    ck_all = cache_diff_k.reshape(depth, Bs, n_cache * H_C, 2 * DH_C)
    cv_all = cache_diff_v.reshape(depth, Bs, n_cache * H_C, DV_C)
    ckv_all = cache_mla_ckv.reshape(depth * Bs * n_cache, KV_LORA)
    sa_all, sb_all = state_hgrn.astype(F32), state_ret.astype(F32)

    def alloc(B, T, tail):
        return jnp.zeros((depth, B, T) + tail, out_dt)

    tails = {"k": (H_C, 2 * DH_C), "v": (H_C, DV_C), "ckv": (KV_LORA,), "kr": (D_ROPE,)}
    p_out = {k: alloc(Bp, N_META + S, t) for k, t in tails.items()}
    s_out = {k: alloc(Bs, Ts, t) for k, t in tails.items()}
    states = {k: [] for k in ("psa", "psb", "ssa", "ssb")}

    for l in range(depth):
        x, x16 = _ffn(x, w1i, w1o, l, row(ln1_g[l]), row(ln1_b[l]), alpha)

        lb2 = jnp.stack([la[l], lc[l]], 0)
        Pa, kc4, vc4, ckr, P16 = _proj(x16, w_in16, l, lb2, row(mla_q_norm_g[l]), row(mla_kv_norm_g[l]), tab)
        q_d = _q_up(P16, w_uq16, l, tab)
        kv_new = _kv_up(P16, ((12 - P16_FIRST) * PROJ_TILE) // KV_LORA, w_ukv16, l)
        kv_past = _kv_up(ckv_all, 0, w_ukv16, l, l * Bs * n_cache, Bs * n_cache).reshape(Bs, n_cache, -1)
        lam_p = jnp.stack([diff_lambda_q1[l], diff_lambda_k1[l], diff_lambda_q2[l], diff_lambda_k2[l]], 0).astype(F32)
        lam_init = 0.8 - 0.6 * math.exp(-0.3 * l)

        y = None
        sa_meta = sb_meta = None
        for name, off, B, T in groups:
            if name == "meta":
                sa0 = sb0 = None
                s_layer = None
                past_c = past_d = None
            elif name == "prompt":
                sa0, sb0, s_layer = sa_meta, sb_meta, None
                past_c = past_d = ("flat", off_m, N_META)
            else:
                sa0, sb0, s_layer = sa_all, sb_all, l
                past_c = ("cache", ck_all, cv_all, l)
                past_d = ("cache", kv_past, cache_mla_krope, l)
            y, sa = _hgrn(Pa, sa0, s_layer, row(hgrn_norm_g[l]), off, B, T, M, y)
            y, sb = _ret(Pa, sb0, s_layer, row(ret_norm_g[l]), off, B, T, M, y)
            y = _diff_attn(P16, past_c, lam_p, row(diff_norm_g[l]), off, B, T, M, lam_init, y)
            y = _mla_attn(q_d, kv_new, P16, past_d, off, B, T, M, y)
            if name == "meta":
                sa_meta, sb_meta = sa, sb
            elif name == "prompt":
                states["psa"].append(sa); states["psb"].append(sb)
            else:
                states["ssa"].append(sa); states["ssb"].append(sb)

        merged = _merge(x16, y, w_gate16, w_branch16, l)
        x, x16 = _mix_out(x, merged, w_out16, l, row(ln2_g[l]), row(ln2_b[l]), alpha)
        x, x16 = _ffn(x, w2i, w2o, l, row(ln3_g[l]), row(ln3_b[l]), alpha)

        pieces = {"k": kc4.reshape(M, H_C, 2 * DH_C), "v": vc4.reshape(M, H_C, DV_C),
                  "ckv": ckr[:, :KV_LORA], "kr": ckr[:, KV_LORA:KV_LORA + D_ROPE]}
        for key, piece in pieces.items():
            tail = tails[key]
            p_out[key] = p_out[key].at[l, :, :N_META].set(piece[off_m:].reshape((Bp, N_META) + tail))
            p_out[key] = p_out[key].at[l, :, N_META:].set(piece[:off_s].reshape((Bp, S) + tail))
            s_out[key] = s_out[key].at[l].set(piece[off_s:off_m].reshape((Bs, Ts) + tail))

    y_prompt = x[:off_s].reshape(Bp, S, D)
    y_sample = x[off_s:off_m].reshape(Bs, Ts, D)
    st = lambda k: jnp.stack(states[k], 0).astype(out_dt)
    return (y_prompt, y_sample, p_out["k"], p_out["v"], p_out["ckv"], p_out["kr"], st("psa"), st("psb"),
            s_out["k"], s_out["v"], s_out["ckv"], s_out["kr"], st("ssa"), st("ssb"))
```

```python
import functools
import math

import numpy as np
import jax
import jax.numpy as jnp
from jax import lax
from jax.experimental import pallas as pl
from jax.experimental.pallas import tpu as pltpu

F32 = jnp.float32
BF16 = jnp.bfloat16

D_MODEL = 2048
CHUNK = 64
N_META = 16
N_BRANCH = 4
BRANCH_W = 512
H_A, DK_A, DV_A = 4, 128, 128
H_B, DK_B, DV_B = 4, 128, 128
RET_THETA = 10000.0
H_C, DH_C, DV_C = 4, 64, 128
PARTIAL_ROT = DH_C // 4
H_D = 4
Q_LORA, KV_LORA = 512, 256
D_NOPE, D_ROPE, DV_D = 128, 64, 128
D_FF = 5632
ROPE_THETA = 500000.0
EPS = 1e-5

LANES = 128
SUBLANES = 8
PROJ_TILE = 512
PROJ_COLS = 13 * PROJ_TILE
P16_FIRST = 8
P16_COLS = PROJ_COLS - P16_FIRST * PROJ_TILE
PA_TILES = 8
VMEM_LIMIT = 56 * 1024 * 1024
NEG_BIG = -1e30
LOG2E = math.log2(math.e)


def _cparams(sem):
    return pltpu.CompilerParams(dimension_semantics=sem, vmem_limit_bytes=VMEM_LIMIT)


def _pick_tile(n, cap, mult=16):
    best = None
    for t in range(mult, min(n, cap) + 1, mult):
        if n % t == 0:
            best = t
    return best if best is not None else n


def _dot(a, b):
    return jnp.dot(a, b, preferred_element_type=F32)


def _dot_nt(a, b):
    return lax.dot_general(a, b, (((1,), (1,)), ((), ())), preferred_element_type=F32)


def _dot_tn(a, b):
    return lax.dot_general(a, b, (((0,), (0,)), ((), ())), preferred_element_type=F32)


def _layer_norm(z, g, b):
    mu = jnp.mean(z, -1, keepdims=True)
    d = z - mu
    var = jnp.mean(d * d, -1, keepdims=True)
    return d * lax.rsqrt(var + EPS) * g + b


def _rms_norm(z, g):
    return z * lax.rsqrt(jnp.mean(z * z, -1, keepdims=True) + EPS) * g


def _silu(a):
    return a * jax.nn.sigmoid(a)


def _rope128(x, c, s1, s2, shift):
    return x * c + pltpu.roll(x, shift, 1) * s1 + pltpu.roll(x, LANES - shift, 1) * s2


def _lower_bound_kernel(logit_ref, la_ref, lc_ref):
    z = logit_ref[...]
    depth = z.shape[0]
    m = z[0:1]
    for l in range(1, depth):
        m = jnp.maximum(m, z[l:l + 1])
    e = jnp.exp(z - m)
    tot = e[0:1]
    for l in range(1, depth):
        tot = tot + e[l:l + 1]
    p = e / tot
    run = jnp.zeros_like(m)
    for l in range(depth):
        la_ref[l:l + 1, :] = jnp.log(run)
        lc_ref[l:l + 1, :] = jnp.log1p(-run)
        run = run + p[l:l + 1]


def _lower_bounds(lb_logits):
    shp = jax.ShapeDtypeStruct(lb_logits.shape, F32)
    return pl.pallas_call(_lower_bound_kernel, out_shape=(shp, shp), name="lower_bounds")(
        lb_logits.astype(F32))


def _ffn_kernel(x_ref, wa_ref, wb_ref, wo_ref, g_ref, b_ref, y32_ref, y16_ref, xs_ref, acc_ref, *, alpha):
    f = pl.program_id(1)

    @pl.when(f == 0)
    def _():
        xs_ref[...] = x_ref[...].astype(BF16)
        acc_ref[...] = jnp.zeros_like(acc_ref)

    x = xs_ref[...]
    a = _dot(x, wa_ref[...])
    b = _dot(x, wb_ref[...])
    h = (_silu(a) * b).astype(BF16)
    acc_ref[...] += _dot(h, wo_ref[...])

    @pl.when(f == pl.num_programs(1) - 1)
    def _():
        y = _layer_norm(alpha * x_ref[...] + 0.5 * acc_ref[...], g_ref[...], b_ref[...])
        y32_ref[...] = y
        y16_ref[...] = y.astype(BF16)


def _ffn(x32, w_in16, w_out16, l, g, b, alpha):
    M, D = x32.shape
    F = w_out16.shape[1]
    tm = _pick_tile(M, 640)
    tf = _pick_tile(F, 512, LANES)
    nf = F // tf
    return pl.pallas_call(
        functools.partial(_ffn_kernel, alpha=alpha),
        grid=(M // tm, nf),
        in_specs=[
            pl.BlockSpec((tm, D), lambda i, f: (i, 0)),
            pl.BlockSpec((None, D, tf), lambda i, f: (l, 0, f)),
            pl.BlockSpec((None, D, tf), lambda i, f: (l, 0, nf + f)),
            pl.BlockSpec((None, tf, D), lambda i, f: (l, f, 0)),
            pl.BlockSpec((1, D), lambda i, f: (0, 0)),
            pl.BlockSpec((1, D), lambda i, f: (0, 0)),
        ],
        out_specs=(pl.BlockSpec((tm, D), lambda i, f: (i, 0)),
                   pl.BlockSpec((tm, D), lambda i, f: (i, 0))),
        out_shape=(jax.ShapeDtypeStruct((M, D), F32), jax.ShapeDtypeStruct((M, D), BF16)),
        scratch_shapes=[pltpu.VMEM((tm, D), BF16), pltpu.VMEM((tm, D), F32)],
        compiler_params=_cparams(("parallel", "arbitrary")),
        name="ffn",
    )(x32, w_in16, w_in16, w_out16, g, b)


def _proj_kernel(x_ref, w_ref, lb_ref, qg_ref, kvg_ref, tab_ref, pa_ref, kc_ref, vc_ref, ckr_ref, o16_ref):
    j = pl.program_id(1)
    tm = x_ref.shape[0]

    def acc():
        return _dot(x_ref[...], w_ref[...])

    def tab(k):
        return tab_ref[:, k * LANES:(k + 1) * LANES]

    def rope_heads(t0, shift):
        c, s1, s2 = tab(t0), tab(t0 + 1), tab(t0 + 2)
        a = acc()
        return [_rope128(a[:, h * LANES:(h + 1) * LANES], c, s1, s2, shift) for h in range(PROJ_TILE // LANES)]

    @pl.when((j == 0) | (j == 3) | (j == 7))
    def _():
        pa_ref[...] = _silu(acc())

    @pl.when(j == 1)
    def _():
        z = acc()
        log_sig = -(jnp.maximum(-z, 0.0) + jnp.log1p(jnp.exp(-jnp.abs(z))))
        a = lb_ref[0:1, :]
        c = lb_ref[1:2, :] + log_sig
        pa_ref[...] = jnp.maximum(a, c) + jnp.log1p(jnp.exp(-jnp.abs(a - c)))

    @pl.when((j == 2) | (j == 6))
    def _():
        pa_ref[...] = acc()

    @pl.when(j == 4)
    def _():
        for h, r in enumerate(rope_heads(0, DK_B // 2)):
            pa_ref[:, h * LANES:(h + 1) * LANES] = r

    @pl.when(j == 5)
    def _():
        for h, r in enumerate(rope_heads(0, DK_B // 2)):
            pa_ref[:, h * LANES:(h + 1) * LANES] = r * DK_B ** -0.5

    @pl.when(j == 8)
    def _():
        for h, r in enumerate(rope_heads(3, PARTIAL_ROT // 2)):
            o16_ref[:, h * LANES:(h + 1) * LANES] = r.astype(BF16)

    @pl.when(j == 9)
    def _():
        for h, r in enumerate(rope_heads(3, PARTIAL_ROT // 2)):
            kc_ref[pl.ds(h, tm, stride=H_C), :] = r
            o16_ref[:, h * LANES:(h + 1) * LANES] = r.astype(BF16)

    @pl.when(j == 10)
    def _():
        a = acc()
        for h in range(H_C):
            vc_ref[pl.ds(h, tm, stride=H_C), :] = a[:, h * LANES:(h + 1) * LANES]
        o16_ref[...] = a.astype(BF16)

    @pl.when(j == 11)
    def _():
        o16_ref[...] = _rms_norm(acc(), qg_ref[...]).astype(BF16)

    @pl.when(j == 12)
    def _():
        a = acc()
        ckv = _rms_norm(a[:, :KV_LORA], kvg_ref[...])
        kr = _rope128(a[:, KV_LORA:KV_LORA + LANES], tab(6), tab(7), tab(8), D_ROPE // 2)
        zeros = jnp.zeros((tm, PROJ_TILE - KV_LORA - LANES), F32)
        ckr_ref[:, :KV_LORA] = ckv
        ckr_ref[:, KV_LORA:KV_LORA + LANES] = kr
        ckr_ref[:, KV_LORA + LANES:] = zeros
        o16_ref[:, :KV_LORA] = ckv.astype(BF16)
        o16_ref[:, KV_LORA:KV_LORA + LANES] = kr.astype(BF16)
        o16_ref[:, KV_LORA + LANES:] = zeros.astype(BF16)


def _proj(x16, w16, l, lb2, qg, kvg, tab):
    M, D = x16.shape
    tm = _pick_tile(M, 1024)
    nj = PROJ_COLS // PROJ_TILE
    once = lambda i, j: (i, 0)
    return pl.pallas_call(
        _proj_kernel,
        grid=(M // tm, nj),
        in_specs=[
            pl.BlockSpec((tm, D), once),
            pl.BlockSpec((None, D, PROJ_TILE), lambda i, j: (l, 0, j)),
            pl.BlockSpec((2, PROJ_TILE), lambda i, j: (0, 0)),
            pl.BlockSpec((1, Q_LORA), lambda i, j: (0, 0)),
            pl.BlockSpec((1, KV_LORA), lambda i, j: (0, 0)),
            pl.BlockSpec((tm, 9 * LANES), once),
        ],
        out_specs=(pl.BlockSpec((tm, PROJ_TILE), lambda i, j: (i, jnp.minimum(j, PA_TILES - 1))),
                   pl.BlockSpec((tm * H_C, LANES), once),
                   pl.BlockSpec((tm * H_C, LANES), once),
                   pl.BlockSpec((tm, PROJ_TILE), once),
                   pl.BlockSpec((tm, PROJ_TILE), lambda i, j: (i, jnp.maximum(j - P16_FIRST, 0)))),
        out_shape=(jax.ShapeDtypeStruct((M, PA_TILES * PROJ_TILE), F32),
                   jax.ShapeDtypeStruct((M * H_C, LANES), F32),
                   jax.ShapeDtypeStruct((M * H_C, LANES), F32),
                   jax.ShapeDtypeStruct((M, PROJ_TILE), F32),
                   jax.ShapeDtypeStruct((M, P16_COLS), BF16)),
        compiler_params=_cparams(("parallel", "arbitrary")),
        name="proj",
    )(x16, w16, lb2, qg, kvg, tab)


def _q_up_kernel(x_ref, w_ref, tab_ref, o_ref):
    acc = _dot(x_ref[...], w_ref[...])
    nope = H_D * D_NOPE
    o_ref[:, :nope] = acc[:, :nope].astype(BF16)
    c, s1, s2 = (tab_ref[:, k * LANES:(k + 1) * LANES] for k in range(3))
    for h in range(H_D):
        lo = nope + h * LANES
        o_ref[:, lo:lo + LANES] = _rope128(acc[:, lo:lo + LANES], c, s1, s2, D_ROPE // 2).astype(BF16)


def _q_up(P16, w16, l, tab):
    M = P16.shape[0]
    N = w16.shape[2]
    tm = _pick_tile(M, 640)
    return pl.pallas_call(
        _q_up_kernel,
        grid=(M // tm,),
        in_specs=[
            pl.BlockSpec((tm, Q_LORA), lambda i: (i, 11 - P16_FIRST)),
            pl.BlockSpec((None, Q_LORA, N), lambda i: (l, 0, 0)),
            pl.BlockSpec((tm, 3 * LANES), lambda i: (i, 2)),
        ],
        out_specs=pl.BlockSpec((tm, N), lambda i: (i, 0)),
        out_shape=jax.ShapeDtypeStruct((M, N), BF16),
        compiler_params=_cparams(("parallel",)),
        name="q_up",
    )(P16, w16, tab)


def _kv_up_kernel(x_ref, w_ref, o_ref):
    o_ref[...] = _dot(x_ref[...].astype(BF16), w_ref[...]).astype(BF16)


def _kv_up(x, col_block, w16, l):
    M = x.shape[0]
    N = w16.shape[2]
    tm = _pick_tile(M, 1024)
    return pl.pallas_call(
        _kv_up_kernel,
        grid=(M // tm,),
        in_specs=[
            pl.BlockSpec((tm, KV_LORA), lambda i: (i, col_block)),
            pl.BlockSpec((None, KV_LORA, N), lambda i: (l, 0, 0)),
        ],
        out_specs=pl.BlockSpec((tm, N), lambda i: (i, 0)),
        out_shape=jax.ShapeDtypeStruct((M, N), BF16),
        compiler_params=_cparams(("parallel",)),
        name="kv_up",
    )(x, w16)


def _merge_kernel(h_ref, y_ref, wg_ref, wb_ref, o_ref, acc_ref):
    n = pl.program_id(2)

    @pl.when(n == 0)
    def _():
        acc_ref[...] = jnp.zeros_like(acc_ref)

    gate = jax.nn.sigmoid(_dot(h_ref[...], wg_ref[...]))
    acc_ref[...] += gate * _dot(y_ref[...], wb_ref[...])

    @pl.when(n == pl.num_programs(2) - 1)
    def _():
        o_ref[...] = acc_ref[...].astype(BF16)


def _merge(h16, y, wg16, wb16, l):
    M, D = h16.shape
    W = BRANCH_W
    tm = _pick_tile(M, 640)
    tn = 1024
    return pl.pallas_call(
        _merge_kernel,
        grid=(M // tm, D // tn, N_BRANCH),
        in_specs=[
            pl.BlockSpec((tm, D), lambda i, j, n: (i, 0)),
            pl.BlockSpec((tm, W), lambda i, j, n: (i, n)),
            pl.BlockSpec((None, None, D, tn), lambda i, j, n: (l, n, 0, j)),
            pl.BlockSpec((None, None, W, tn), lambda i, j, n: (l, n, 0, j)),
        ],
        out_specs=pl.BlockSpec((tm, tn), lambda i, j, n: (i, j)),
        out_shape=jax.ShapeDtypeStruct((M, D), BF16),
        scratch_shapes=[pltpu.VMEM((tm, tn), F32)],
        compiler_params=_cparams(("parallel", "arbitrary", "arbitrary")),
        name="merge",
    )(h16, y, wg16, wb16)


def _mix_out_kernel(x_ref, m_ref, w_ref, g_ref, b_ref, y32_ref, y16_ref, *, alpha):
    y = _layer_norm(alpha * x_ref[...] + _dot(m_ref[...], w_ref[...]), g_ref[...], b_ref[...])
    y32_ref[...] = y
    y16_ref[...] = y.astype(BF16)


def _mix_out(x32, merged16, w16, l, g, b, alpha):
    M, D = x32.shape
    tm = _pick_tile(M, 640)
    return pl.pallas_call(
        functools.partial(_mix_out_kernel, alpha=alpha),
        grid=(M // tm,),
        in_specs=[
            pl.BlockSpec((tm, D), lambda i: (i, 0)),
            pl.BlockSpec((tm, D), lambda i: (i, 0)),
            pl.BlockSpec((None, D, D), lambda i: (l, 0, 0)),
            pl.BlockSpec((1, D), lambda i: (0, 0)),
            pl.BlockSpec((1, D), lambda i: (0, 0)),
        ],
        out_specs=(pl.BlockSpec((tm, D), lambda i: (i, 0)),
                   pl.BlockSpec((tm, D), lambda i: (i, 0))),
        out_shape=(jax.ShapeDtypeStruct((M, D), F32), jax.ShapeDtypeStruct((M, D), BF16)),
        compiler_params=_cparams(("parallel",)),
        name="mix_out",
    )(x32, merged16, w16, g, b)


def _y_alias(y_prev, n_inputs):
    if y_prev is None:
        return [], [], {}
    return [pl.BlockSpec(memory_space=pl.ANY)], [y_prev], {n_inputs: 0}


def _state_spec(s0, layer, H, DK, DV):
    if layer is None:
        return pl.BlockSpec((None, H, DK, DV), lambda b, c: (b, 0, 0, 0))
    return pl.BlockSpec((None, None, H, DK, DV), lambda b, c: (layer, b, 0, 0, 0))


def _hgrn_kernel(*refs, C, has_s0):
    q_ref, lf_ref, v_ref, gate_ref = refs[:4]
    s0_ref = refs[4] if has_s0 else None
    g_ref = refs[4 + has_s0]
    y_ref, s_out_ref, st_ref = refs[-3:]
    step = pl.program_id(1)
    n_sub = q_ref.shape[0] // C
    nv = C // SUBLANES

    @pl.when(step == 0)
    def _():
        for h in range(H_A):
            st_ref[h] = s0_ref[h].T if has_s0 else jnp.zeros((DV_A, DK_A), F32)

    row8 = lax.broadcasted_iota(jnp.int32, (SUBLANES, LANES), 0)
    lane8 = lax.broadcasted_iota(jnp.int32, (SUBLANES, C), 1)
    tril = (lax.broadcasted_iota(jnp.int32, (C, C), 1) <= lax.broadcasted_iota(jnp.int32, (C, C), 0)).astype(F32)

    def chunk(ci, carry):
        rs = pl.ds(pl.multiple_of(ci * C, C), C)
        heads = range(H_A)
        sls = [slice(h * LANES, (h + 1) * LANES) for h in heads]
        b_all = jnp.dot(tril, lf_ref[rs, :], preferred_element_type=F32, precision=lax.Precision.HIGHEST)
        q = [q_ref[rs, sl] for sl in sls]
        v16 = [v_ref[rs, sl].astype(BF16) for sl in sls]
        k = [1.0 - jnp.exp(lf_ref[rs, sl]) for sl in sls]
        b = [b_all[:, sl] for sl in sls]
        st = [st_ref[h] for h in heads]
        o = [_dot_nt((q[h] * jnp.exp(b[h])).astype(BF16), st[h].astype(BF16)) for h in heads]
        res = []
        for h in heads:
            b2 = b[h] * LOG2E
            pieces = []
            for s in range(C):
                r0 = (s // SUBLANES) * SUBLANES
                d = b2[r0:] - b2[s:s + 1]
                head = jnp.where(row8 >= s % SUBLANES, d[:SUBLANES], -jnp.inf)
                d = head if C - r0 == SUBLANES else jnp.concatenate([head, d[SUBLANES:]], 0)
                pieces.append(q[h][r0:] * jnp.exp2(d))
            res.append(_dot_nt(jnp.concatenate(pieces, 0).astype(BF16), k[h].astype(BF16)))
        for h in heads:
            a_parts = [jnp.zeros((SUBLANES, C), F32) for _ in range(nv)]
            off = 0
            for s in range(C):
                for i in range(s // SUBLANES, nv):
                    a_parts[i] = a_parts[i] + jnp.where(lane8 == s, res[h][off:off + SUBLANES], 0.0)
                    off += SUBLANES
            o[h] = o[h] + _dot(jnp.concatenate(a_parts, 0).astype(BF16), v16[h])
        for h in heads:
            b_last = b[h][C - 1:C, :]
            kd = (k[h] * jnp.exp(b_last - b[h])).astype(BF16)
            st_ref[h] = st[h] * jnp.exp(b_last) + _dot_tn(v16[h], kd)
            y_ref[rs, sls[h]] = (_rms_norm(o[h], g_ref[...]) * gate_ref[rs, sls[h]]).astype(BF16)
        return carry

    lax.fori_loop(0, n_sub, chunk, 0)

    @pl.when(step == pl.num_programs(1) - 1)
    def _():
        for h in range(H_A):
            s_out_ref[h] = st_ref[h].T


def _hgrn(Pa, s0, s0_layer, g, off, B, T, M, y_prev):
    C = min(T, 64)
    rows_blk = min(T, 256)
    nb = T // rows_blk
    base = off // rows_blk
    width = H_A * DV_A

    def rows(colblk):
        return pl.BlockSpec((rows_blk, width), lambda b, c: (base + b * nb + c, colblk))

    s_specs, s_args = ([], []) if s0 is None else ([_state_spec(s0, s0_layer, H_A, DK_A, DV_A)], [s0])
    a_specs, a_args, aliases = _y_alias(y_prev, 5 + len(s_args))
    return pl.pallas_call(
        functools.partial(_hgrn_kernel, C=C, has_s0=bool(s_args)),
        grid=(B, nb),
        in_specs=[rows(0), rows(1), rows(2), rows(3)] + s_specs + [pl.BlockSpec((1, DV_A), lambda b, c: (0, 0))] + a_specs,
        out_specs=(pl.BlockSpec((rows_blk, width), lambda b, c: (base + b * nb + c, 0)),
                   pl.BlockSpec((None, H_A, DK_A, DV_A), lambda b, c: (b, 0, 0, 0))),
        out_shape=(jax.ShapeDtypeStruct((M, N_BRANCH * BRANCH_W), BF16),
                   jax.ShapeDtypeStruct((B, H_A, DK_A, DV_A), F32)),
        scratch_shapes=[pltpu.VMEM((H_A, DV_A, DK_A), F32)],
        input_output_aliases=aliases,
        compiler_params=_cparams(("parallel", "arbitrary")),
        name="hgrn",
    )(Pa, Pa, Pa, Pa, *s_args, g, *a_args)


def _ret_kernel(*refs, has_s0):
    q_ref, k_ref, v_ref, gate_ref = refs[:4]
    s0_ref = refs[4] if has_s0 else None
    g_ref = refs[4 + has_s0]
    y_ref, s_out_ref, s_ref = refs[-3:]
    c = pl.program_id(1)
    C = q_ref.shape[0]

    @pl.when(c == 0)
    def _():
        s_ref[...] = s0_ref[...] if has_s0 else jnp.zeros_like(s_ref)

    ti = lax.broadcasted_iota(jnp.int32, (C, C), 0)
    si = lax.broadcasted_iota(jnp.int32, (C, C), 1)
    dist = (ti - si).astype(F32)
    t1 = (lax.broadcasted_iota(jnp.int32, (C, 1), 0) + 1).astype(F32)

    heads = range(H_B)
    sls = [slice(h * LANES, (h + 1) * LANES) for h in heads]
    log_gamma = [math.log1p(-(2.0 ** (-5.0 - h))) for h in heads]
    q = [q_ref[:, sl] for sl in sls]
    k = [k_ref[:, sl] for sl in sls]
    v16 = [v_ref[:, sl].astype(BF16) for sl in sls]
    s = [s_ref[h] for h in heads]
    a = [_dot_nt(q[h].astype(BF16), k[h].astype(BF16))
         * jnp.exp(jnp.where(ti >= si, dist * log_gamma[h], -jnp.inf)) for h in heads]
    o = [_dot(a[h].astype(BF16), v16[h])
         + _dot((q[h] * jnp.exp(t1 * log_gamma[h])).astype(BF16), s[h].astype(BF16)) for h in heads]
    for h in heads:
        kd = (k[h] * jnp.exp((C - t1) * log_gamma[h])).astype(BF16)
        s_ref[h] = math.exp(C * log_gamma[h]) * s[h] + _dot_tn(kd, v16[h])
    for h in heads:
        mu = jnp.mean(o[h], -1, keepdims=True)
        d = o[h] - mu
        var = jnp.mean(d * d, -1, keepdims=True)
        y = d * lax.rsqrt(var + EPS) * g_ref[...] * gate_ref[:, sls[h]]
        y_ref[:, sls[h]] = y.astype(BF16)

    @pl.when(c == pl.num_programs(1) - 1)
    def _():
        s_out_ref[...] = s_ref[...]


def _ret(Pa, s0, s0_layer, g, off, B, T, M, y_prev):
    C = min(T, 128)
    nc = T // C
    base = off // C
    width = H_B * DV_B

    def rows(colblk):
        return pl.BlockSpec((C, width), lambda b, c: (base + b * nc + c, colblk))

    s_specs, s_args = ([], []) if s0 is None else ([_state_spec(s0, s0_layer, H_B, DK_B, DV_B)], [s0])
    a_specs, a_args, aliases = _y_alias(y_prev, 5 + len(s_args))
    return pl.pallas_call(
        functools.partial(_ret_kernel, has_s0=bool(s_args)),
        grid=(B, nc),
        in_specs=[rows(4), rows(5), rows(6), rows(7)] + s_specs + [pl.BlockSpec((1, DV_B), lambda b, c: (0, 0))] + a_specs,
        out_specs=(pl.BlockSpec((C, width), lambda b, c: (base + b * nc + c, 1)),
                   pl.BlockSpec((None, H_B, DK_B, DV_B), lambda b, c: (b, 0, 0, 0))),
        out_shape=(jax.ShapeDtypeStruct((M, N_BRANCH * BRANCH_W), BF16),
                   jax.ShapeDtypeStruct((B, H_B, DK_B, DV_B), F32)),
        scratch_shapes=[pltpu.VMEM((H_B, DK_B, DV_B), F32)],
        input_output_aliases=aliases,
        compiler_params=_cparams(("parallel", "arbitrary")),
        name="retention",
    )(Pa, Pa, Pa, Pa, *s_args, g, *a_args)


def _past_blocks(n, cap=512):
    out, r = [], 0
    while r < n:
        w = min(cap, n - r)
        out.append((r, w))
        r += w
    return out


def _colmax(mrun, s):
    n = s.shape[1]
    if n % LANES:
        return jnp.maximum(mrun, jnp.max(s, -1, keepdims=True))
    for gi in range(n // LANES):
        mrun = jnp.maximum(mrun, s[:, gi * LANES:(gi + 1) * LANES])
    return mrun


def _attend(n_heads, score_fn, value_fn, n_past, bq, qi, bias, sp_ref, ss_ref, m_ref, acc_ref):
    blocks = _past_blocks(n_past)
    heads = range(n_heads)
    m_ref[...] = jnp.full(m_ref.shape, NEG_BIG, F32)

    def keep(h, dst, idx, s):
        dst[idx] = s
        m_ref[h] = _colmax(m_ref[h], s)

    for r0, n in blocks:
        for h in heads:
            keep(h, sp_ref, (h, slice(None), slice(r0, r0 + n)), score_fn(h, "past", r0, n))

    def scores(j, carry):
        r0 = pl.multiple_of(j * bq, bq)
        for h in heads:
            keep(h, ss_ref, (h, j), score_fn(h, "self", r0, bq))
        return carry

    lax.fori_loop(0, qi, scores, 0)
    r_diag = pl.multiple_of(qi * bq, bq)
    for h in heads:
        s = score_fn(h, "self", r_diag, bq)
        keep(h, ss_ref, (h, qi), s if bias is None else s + bias)
    m = [jnp.max(m_ref[h], -1, keepdims=True) for h in heads]

    acc_ref[...] = jnp.zeros_like(acc_ref)

    def accumulate(h, s, v16):
        p = jnp.exp2(s - m[h]).astype(BF16)
        acc_ref[h] += _dot(p, jnp.concatenate([v16, jnp.ones_like(v16)], 1))

    for r0, n in blocks:
        for h in heads:
            accumulate(h, sp_ref[h, :, r0:r0 + n], value_fn(h, "past", r0, n))

    def weighted(j, carry):
        r0 = pl.multiple_of(j * bq, bq)
        for h in heads:
            accumulate(h, ss_ref[h, j], value_fn(h, "self", r0, bq))
        return carry

    lax.fori_loop(0, qi, weighted, 0)
    for h in heads:
        accumulate(h, ss_ref[h, qi], value_fn(h, "self", r_diag, bq))
    dv = acc_ref.shape[2] // 2
    return [acc_ref[h, :, :dv] / acc_ref[h, :, dv:] for h in heads]


def _chunk_bias(rows, bq):
    if bq <= CHUNK:
        return None
    qc = (lax.broadcasted_iota(jnp.int32, (rows, bq), 0) % bq) // CHUNK
    kc = lax.broadcasted_iota(jnp.int32, (rows, bq), 1) // CHUNK
    return jnp.where(kc <= qc, 0.0, -jnp.inf).astype(F32)


def _attn_scratch(n_heads, rows, n_past, nq, bq, dv):
    return [pltpu.VMEM((n_heads, rows, max(LANES, -(-n_past // LANES) * LANES)), F32),
            pltpu.VMEM((n_heads, nq, rows, bq), F32),
            pltpu.VMEM((n_heads, rows, LANES), F32),
            pltpu.VMEM((n_heads, rows, 2 * dv), F32)]


def _diff_kernel(*refs, past_mode, n_past, lam_init):
    refs = list(refs)
    sp_ref, ss_ref, m_ref, acc_ref = refs[-4:]
    y_ref = refs[-5]
    if past_mode:
        q_ref, ks_ref, vs_ref, kp_ref, vp_ref, lam_ref, g_ref = refs[:7]
    else:
        q_ref, ks_ref, vs_ref, lam_ref, g_ref = refs[:5]
        kp_ref = vp_ref = None
    qi = pl.program_id(1)
    bq = q_ref.shape[0]
    lam_p = lam_ref[...]
    lam = (jnp.exp(jnp.sum(lam_p[0:1] * lam_p[1:2], -1, keepdims=True))
           - jnp.exp(jnp.sum(lam_p[2:3] * lam_p[3:4], -1, keepdims=True)) + lam_init)
    lane = lax.broadcasted_iota(jnp.int32, (bq, LANES), 1)
    bias = _chunk_bias(2 * bq, bq)
    scale = DH_C ** -0.5 * LOG2E
    sls = [slice(h * LANES, (h + 1) * LANES) for h in range(H_C)]
    q2 = []
    for sl in sls:
        qh = q_ref[:, sl]
        zero = jnp.zeros_like(qh)
        q2.append(jnp.concatenate([jnp.where(lane < DH_C, qh, zero), jnp.where(lane >= DH_C, qh, zero)], 0))

    def rows_of(ref, h, src, r0, n):
        if src == "past" and past_mode == "cache":
            return ref[pl.ds(H_C * r0 + h, n, stride=H_C), :]
        return ref[pl.ds(r0, n), sls[h]]

    def score_fn(h, src, r0, n):
        k = rows_of(kp_ref if src == "past" else ks_ref, h, src, r0, n)
        return _dot_nt(q2[h], k.astype(BF16)) * scale

    def value_fn(h, src, r0, n):
        return rows_of(vp_ref if src == "past" else vs_ref, h, src, r0, n).astype(BF16)

    a = _attend(H_C, score_fn, value_fn, n_past, bq, qi, bias, sp_ref, ss_ref, m_ref, acc_ref)
    for h, sl in enumerate(sls):
        d = a[h][:bq] - lam * a[h][bq:]
        y_ref[:, sl] = (_rms_norm(d, g_ref[...]) * (1.0 - lam_init)).astype(BF16)


def _q_block(T):
    for bq in (256, 128):
        if T % bq == 0:
            return bq
    return T


def _diff_attn(P16, past, lam_p, g, off, B, T, M, lam_init, y_prev):
    bq = _q_block(T)
    nq = T // bq
    width = H_C * DV_C
    in_specs = [
        pl.BlockSpec((bq, width), lambda b, i: (off // bq + b * nq + i, 0)),
        pl.BlockSpec((T, width), lambda b, i: (off // T + b, 1)),
        pl.BlockSpec((T, width), lambda b, i: (off // T + b, 2)),
    ]
    args = [P16, P16, P16]
    n_past, past_mode = 0, None
    if past is not None:
        past_mode = past[0]
        if past_mode == "flat":
            _, poff, n_past = past
            in_specs += [pl.BlockSpec((n_past, width), lambda b, i: (poff // n_past + b, 1)),
                         pl.BlockSpec((n_past, width), lambda b, i: (poff // n_past + b, 2))]
            args += [P16, P16]
        else:
            _, pk, pv, layer = past
            n_past = pk.shape[2] // H_C
            spec = pl.BlockSpec((None, None, n_past * H_C, LANES), lambda b, i: (layer, b, 0, 0))
            in_specs += [spec, spec]
            args += [pk, pv]
    in_specs += [pl.BlockSpec((4, DH_C), lambda b, i: (0, 0)), pl.BlockSpec((1, DV_C), lambda b, i: (0, 0))]
    args += [lam_p, g]
    a_specs, a_args, aliases = _y_alias(y_prev, len(args))
    return pl.pallas_call(
        functools.partial(_diff_kernel, past_mode=past_mode, n_past=n_past, lam_init=lam_init),
        grid=(B, nq),
        in_specs=in_specs + a_specs,
        out_specs=pl.BlockSpec((bq, width), lambda b, i: (off // bq + b * nq + i, 2)),
        out_shape=jax.ShapeDtypeStruct((M, N_BRANCH * BRANCH_W), BF16),
        scratch_shapes=_attn_scratch(H_C, 2 * bq, n_past, nq, bq, DV_C),
        input_output_aliases=aliases,
        compiler_params=_cparams(("parallel", "arbitrary")),
        name="diff_attn",
    )(*args, *a_args)


def _mla_kernel(*refs, past_mode, n_past):
    refs = list(refs)
    qi = pl.program_id(1)
    if past_mode == "cache":
        sp_ref, ss_ref, m_ref, acc_ref, kvp_ref = refs[-5:]
        y_ref = refs[-6]
        q_ref, kvs_ref, krs_ref, ckvp_ref, krp_ref, wkv_ref = refs[:6]

        @pl.when(qi == 0)
        def _():
            for r0, n in _past_blocks(n_past):
                kvp_ref[r0:r0 + n, :] = _dot(ckvp_ref[r0:r0 + n, :].astype(BF16), wkv_ref[...]).astype(BF16)
    else:
        sp_ref, ss_ref, m_ref, acc_ref = refs[-4:]
        y_ref = refs[-5]
        if past_mode:
            q_ref, kvs_ref, krs_ref, kvp_ref, krp_ref = refs[:5]
        else:
            q_ref, kvs_ref, krs_ref = refs[:3]
            kvp_ref = krp_ref = None
    bq = q_ref.shape[0]
    bias = _chunk_bias(bq, bq)
    scale = (D_NOPE + D_ROPE) ** -0.5 * LOG2E
    qn = [q_ref[:, h * D_NOPE:(h + 1) * D_NOPE] for h in range(H_D)]
    qr = [q_ref[:, H_D * D_NOPE + h * LANES:H_D * D_NOPE + h * LANES + D_ROPE] for h in range(H_D)]

    def score_fn(h, src, r0, n):
        kv = kvp_ref if src == "past" else kvs_ref
        kr = krp_ref if src == "past" else krs_ref
        ksl = slice(h * (D_NOPE + DV_D), h * (D_NOPE + DV_D) + D_NOPE)
        s = _dot_nt(qn[h], kv[pl.ds(r0, n), ksl]) + _dot_nt(qr[h], kr[pl.ds(r0, n), :D_ROPE].astype(BF16))
        return s * scale

    def value_fn(h, src, r0, n):
        vsl = slice(h * (D_NOPE + DV_D) + D_NOPE, (h + 1) * (D_NOPE + DV_D))
        return (kvp_ref if src == "past" else kvs_ref)[pl.ds(r0, n), vsl]

    a = _attend(H_D, score_fn, value_fn, n_past, bq, qi, bias, sp_ref, ss_ref, m_ref, acc_ref)
    for h in range(H_D):
        y_ref[:, h * DV_D:(h + 1) * DV_D] = a[h].astype(BF16)


def _mla_attn(q16, kv16, P16, past, off, B, T, M, y_prev):
    bq = _q_block(T)
    nq = T // bq
    wq = q16.shape[1]
    wkv = kv16.shape[1]
    kr_blk = ((12 - P16_FIRST) * PROJ_TILE + KV_LORA) // LANES
    in_specs = [
        pl.BlockSpec((bq, wq), lambda b, i: (off // bq + b * nq + i, 0)),
        pl.BlockSpec((T, wkv), lambda b, i: (off // T + b, 0)),
        pl.BlockSpec((T, LANES), lambda b, i: (off // T + b, kr_blk)),
    ]
    args = [q16, kv16, P16]
    n_past, past_mode = 0, None
    scratch = []
    if past is not None:
        past_mode = past[0]
        if past_mode == "flat":
            _, poff, n_past = past
            in_specs += [pl.BlockSpec((n_past, wkv), lambda b, i: (poff // n_past + b, 0)),
                         pl.BlockSpec((n_past, LANES), lambda b, i: (poff // n_past + b, kr_blk))]
            args += [kv16, P16]
        else:
            _, pckv, pkr, w_ukv16, layer = past
            n_past = pckv.shape[2]
            in_specs += [pl.BlockSpec((None, None, n_past, KV_LORA), lambda b, i: (layer, b, 0, 0)),
                         pl.BlockSpec((None, None, n_past, D_ROPE), lambda b, i: (layer, b, 0, 0)),
                         pl.BlockSpec((None, KV_LORA, wkv), lambda b, i: (layer, 0, 0))]
            args += [pckv, pkr, w_ukv16]
            scratch = [pltpu.VMEM((n_past, wkv), BF16)]
    width = H_D * DV_D
    a_specs, a_args, aliases = _y_alias(y_prev, len(args))
    return pl.pallas_call(
        functools.partial(_mla_kernel, past_mode=past_mode, n_past=n_past),
        grid=(B, nq),
        in_specs=in_specs + a_specs,
        out_specs=pl.BlockSpec((bq, width), lambda b, i: (off // bq + b * nq + i, 3)),
        out_shape=jax.ShapeDtypeStruct((M, N_BRANCH * BRANCH_W), BF16),
        scratch_shapes=_attn_scratch(H_D, bq, n_past, nq, bq, DV_D) + scratch,
        input_output_aliases=aliases,
        compiler_params=_cparams(("parallel", "arbitrary")),
        name="mla_attn",
    )(*args, *a_args)


def _rope_table(pos, period, half, rot_dim, theta):
    lane = np.arange(LANES)
    li = lane % period
    first = li < half
    second = (li >= half) & (li < rot_dim)
    idx = np.where(first, li, np.where(second, li - half, 0))
    freq = jnp.power(jnp.float32(theta), -jnp.arange(half, dtype=F32) / half)[idx]
    ang = pos.astype(F32)[:, None] * freq[None, :]
    cos, sin = jnp.cos(ang), jnp.sin(ang)
    rot = jnp.asarray(first | second)[None, :]
    return [jnp.where(rot, cos, 1.0), jnp.where(jnp.asarray(second)[None, :], sin, 0.0),
            jnp.where(jnp.asarray(first)[None, :], -sin, 0.0)]


def _rope_tables(pos):
    tabs = (_rope_table(pos, LANES, DK_B // 2, DK_B, RET_THETA)
            + _rope_table(pos, DH_C, PARTIAL_ROT // 2, PARTIAL_ROT, ROPE_THETA)
            + _rope_table(pos, LANES, D_ROPE // 2, D_ROPE, ROPE_THETA))
    return jnp.concatenate(tabs, axis=1)


def _uq_layout(w_uq):
    depth = w_uq.shape[0]
    w = w_uq.reshape(depth, Q_LORA, H_D, D_NOPE + D_ROPE)
    nope = w[..., :D_NOPE].reshape(depth, Q_LORA, H_D * D_NOPE)
    rope = jnp.pad(w[..., D_NOPE:], ((0, 0), (0, 0), (0, 0), (0, LANES - D_ROPE))).reshape(depth, Q_LORA, H_D * LANES)
    return jnp.concatenate([nope, rope], axis=2).astype(BF16)


def kernel(x_prompt, x_sample, cache_diff_k, cache_diff_v, cache_mla_ckv, cache_mla_krope, state_hgrn, state_ret, meta_tokens, w_ffn1_in, w_ffn1_out, ln1_g, ln1_b, w_in, lb_logits, hgrn_norm_g, ret_norm_g, diff_lambda_q1, diff_lambda_k1, diff_lambda_q2, diff_lambda_k2, diff_norm_g, mla_q_norm_g, mla_kv_norm_g, w_mla_uq, w_mla_ukv, w_branch, w_merge_gate, w_mix_out, ln2_g, ln2_b, w_ffn2_in, w_ffn2_out, ln3_g, ln3_b):
    depth = w_in.shape[0]
    Bp, S, D = x_prompt.shape
    Bs, Ts, _ = x_sample.shape
    n_cache = cache_diff_k.shape[2]
    past = n_cache - N_META
    assert S % (2 * CHUNK) == 0 and Ts <= CHUNK and past % CHUNK == 0 and (past + Ts - 1) // CHUNK == past // CHUNK
    alpha = (2 * depth) ** 0.25

    off_s = Bp * S
    off_m = off_s + Bs * Ts
    M = off_m + Bp * N_META
    assert off_s % Ts == 0 and off_m % N_META == 0
    x = jnp.concatenate([
        x_prompt.reshape(Bp * S, D), x_sample.reshape(Bs * Ts, D),
        jnp.broadcast_to(meta_tokens[None].astype(x_prompt.dtype), (Bp, N_META, D)).reshape(Bp * N_META, D)], 0)
    pos = jnp.concatenate([
        jnp.tile(N_META + jnp.arange(S, dtype=jnp.int32), Bp),
        jnp.tile(N_META + past + jnp.arange(Ts, dtype=jnp.int32), Bs),
        jnp.tile(jnp.arange(N_META, dtype=jnp.int32), Bp)])
    tab = _rope_tables(pos)

    w1i, w1o = w_ffn1_in.astype(BF16), w_ffn1_out.astype(BF16)
    w2i, w2o = w_ffn2_in.astype(BF16), w_ffn2_out.astype(BF16)
    w_in16 = jnp.pad(w_in, ((0, 0), (0, 0), (0, PROJ_COLS - w_in.shape[2]))).astype(BF16)
    w_uq16 = _uq_layout(w_mla_uq)
    w_ukv16 = w_mla_ukv.astype(BF16)
    w_gate16, w_branch16, w_out16 = w_merge_gate.astype(BF16), w_branch.astype(BF16), w_mix_out.astype(BF16)

    la, lc = _lower_bounds(lb_logits)
    groups = (("meta", off_m, Bp, N_META), ("prompt", 0, Bp, S), ("sample", off_s, Bs, Ts))
    row = lambda a: a.reshape(1, -1).astype(F32)
    out_dt = x_prompt.dtype

    ck_all = cache_diff_k.reshape(depth, Bs, n_cache * H_C, 2 * DH_C)
    cv_all = cache_diff_v.reshape(depth, Bs, n_cache * H_C, DV_C)
    sa_all, sb_all = state_hgrn.astype(F32), state_ret.astype(F32)

    def alloc(B, T, tail):
        return jnp.zeros((depth, B, T) + tail, out_dt)

    tails = {"k": (H_C, 2 * DH_C), "v": (H_C, DV_C), "ckv": (KV_LORA,), "kr": (D_ROPE,)}
    p_out = {k: alloc(Bp, N_META + S, t) for k, t in tails.items()}
    s_out = {k: alloc(Bs, Ts, t) for k, t in tails.items()}
    states = {k: [] for k in ("psa", "psb", "ssa", "ssb")}

    for l in range(depth):
        x, x16 = _ffn(x, w1i, w1o, l, row(ln1_g[l]), row(ln1_b[l]), alpha)

        lb2 = jnp.stack([la[l], lc[l]], 0)
        Pa, kc4, vc4, ckr, P16 = _proj(x16, w_in16, l, lb2, row(mla_q_norm_g[l]), row(mla_kv_norm_g[l]), tab)
        q_d = _q_up(P16, w_uq16, l, tab)
        kv_new = _kv_up(P16, ((12 - P16_FIRST) * PROJ_TILE) // KV_LORA, w_ukv16, l)
        lam_p = jnp.stack([diff_lambda_q1[l], diff_lambda_k1[l], diff_lambda_q2[l], diff_lambda_k2[l]], 0).astype(F32)
        lam_init = 0.8 - 0.6 * math.exp(-0.3 * l)

        y = None
        sa_meta = sb_meta = None
        for name, off, B, T in groups:
            if name == "meta":
                sa0 = sb0 = None
                s_layer = None
                past_c = past_d = None
            elif name == "prompt":
                sa0, sb0, s_layer = sa_meta, sb_meta, None
                past_c = past_d = ("flat", off_m, N_META)
            else:
                sa0, sb0, s_layer = sa_all, sb_all, l
                past_c = ("cache", ck_all, cv_all, l)
                past_d = ("cache", cache_mla_ckv, cache_mla_krope, w_ukv16, l)
            y, sa = _hgrn(Pa, sa0, s_layer, row(hgrn_norm_g[l]), off, B, T, M, y)
            y, sb = _ret(Pa, sb0, s_layer, row(ret_norm_g[l]), off, B, T, M, y)
            y = _diff_attn(P16, past_c, lam_p, row(diff_norm_g[l]), off, B, T, M, lam_init, y)
            y = _mla_attn(q_d, kv_new, P16, past_d, off, B, T, M, y)
            if name == "meta":
                sa_meta, sb_meta = sa, sb
            elif name == "prompt":
                states["psa"].append(sa); states["psb"].append(sb)
            else:
                states["ssa"].append(sa); states["ssb"].append(sb)

        merged = _merge(x16, y, w_gate16, w_branch16, l)
        x, x16 = _mix_out(x, merged, w_out16, l, row(ln2_g[l]), row(ln2_b[l]), alpha)
        x, x16 = _ffn(x, w2i, w2o, l, row(ln3_g[l]), row(ln3_b[l]), alpha)

        pieces = {"k": kc4.reshape(M, H_C, 2 * DH_C), "v": vc4.reshape(M, H_C, DV_C),
                  "ckv": ckr[:, :KV_LORA], "kr": ckr[:, KV_LORA:KV_LORA + D_ROPE]}
        for key, piece in pieces.items():
            tail = tails[key]
            p_out[key] = p_out[key].at[l, :, :N_META].set(piece[off_m:].reshape((Bp, N_META) + tail))
            p_out[key] = p_out[key].at[l, :, N_META:].set(piece[:off_s].reshape((Bp, S) + tail))
            s_out[key] = s_out[key].at[l].set(piece[off_s:off_m].reshape((Bs, Ts) + tail))

    y_prompt = x[:off_s].reshape(Bp, S, D)
    y_sample = x[off_s:off_m].reshape(Bs, Ts, D)
    st = lambda k: jnp.stack(states[k], 0).astype(out_dt)
    return (y_prompt, y_sample, p_out["k"], p_out["v"], p_out["ckv"], p_out["kr"], st("psa"), st("psb"),
            s_out["k"], s_out["v"], s_out["ckv"], s_out["kr"], st("ssa"), st("ssb"))
```

```python
import functools
import math

import numpy as np
import jax
import jax.numpy as jnp
from jax import lax
from jax.experimental import pallas as pl
from jax.experimental.pallas import tpu as pltpu

F32 = jnp.float32
BF16 = jnp.bfloat16

D_MODEL = 2048
CHUNK = 64
N_META = 16
N_BRANCH = 4
BRANCH_W = 512
H_A, DK_A, DV_A = 4, 128, 128
H_B, DK_B, DV_B = 4, 128, 128
RET_THETA = 10000.0
H_C, DH_C, DV_C = 4, 64, 128
PARTIAL_ROT = DH_C // 4
H_D = 4
Q_LORA, KV_LORA = 512, 256
D_NOPE, D_ROPE, DV_D = 128, 64, 128
D_FF = 5632
ROPE_THETA = 500000.0
EPS = 1e-5

LANES = 128
SUBLANES = 8
PROJ_TILE = 512
PROJ_COLS = 13 * PROJ_TILE
P16_FIRST = 8
P16_COLS = PROJ_COLS - P16_FIRST * PROJ_TILE
PA_TILES = 8
VMEM_LIMIT = 56 * 1024 * 1024
NEG_BIG = -1e30
LOG2E = math.log2(math.e)


def _cparams(sem):
    return pltpu.CompilerParams(dimension_semantics=sem, vmem_limit_bytes=VMEM_LIMIT)


def _pick_tile(n, cap, mult=16):
    best = None
    for t in range(mult, min(n, cap) + 1, mult):
        if n % t == 0:
            best = t
    return best if best is not None else n


def _dot(a, b):
    return jnp.dot(a, b, preferred_element_type=F32)


def _dot_nt(a, b):
    return lax.dot_general(a, b, (((1,), (1,)), ((), ())), preferred_element_type=F32)


def _dot_tn(a, b):
    return lax.dot_general(a, b, (((0,), (0,)), ((), ())), preferred_element_type=F32)


def _layer_norm(z, g, b):
    mu = jnp.mean(z, -1, keepdims=True)
    d = z - mu
    var = jnp.mean(d * d, -1, keepdims=True)
    return d * lax.rsqrt(var + EPS) * g + b


def _rms_norm(z, g):
    return z * lax.rsqrt(jnp.mean(z * z, -1, keepdims=True) + EPS) * g


def _silu(a):
    return a * jax.nn.sigmoid(a)


def _rope128(x, c, s1, s2, shift):
    return x * c + pltpu.roll(x, shift, 1) * s1 + pltpu.roll(x, LANES - shift, 1) * s2


def _lower_bound_kernel(logit_ref, la_ref, lc_ref):
    z = logit_ref[...]
    depth = z.shape[0]
    m = z[0:1]
    for l in range(1, depth):
        m = jnp.maximum(m, z[l:l + 1])
    e = jnp.exp(z - m)
    tot = e[0:1]
    for l in range(1, depth):
        tot = tot + e[l:l + 1]
    p = e / tot
    run = jnp.zeros_like(m)
    for l in range(depth):
        la_ref[l:l + 1, :] = jnp.log(run)
        lc_ref[l:l + 1, :] = jnp.log1p(-run)
        run = run + p[l:l + 1]


def _lower_bounds(lb_logits):
    shp = jax.ShapeDtypeStruct(lb_logits.shape, F32)
    return pl.pallas_call(_lower_bound_kernel, out_shape=(shp, shp), name="lower_bounds")(
        lb_logits.astype(F32))


def _ffn_kernel(x_ref, x16_ref, wa_ref, wb_ref, wo_ref, g_ref, b_ref, y32_ref, y16_ref, acc_ref, *, alpha):
    f = pl.program_id(1)

    @pl.when(f == 0)
    def _():
        acc_ref[...] = jnp.zeros_like(acc_ref)

    x = x16_ref[...]
    a = _dot(x, wa_ref[...])
    b = _dot(x, wb_ref[...])
    h = (_silu(a) * b).astype(BF16)
    acc_ref[...] += _dot(h, wo_ref[...])

    @pl.when(f == pl.num_programs(1) - 1)
    def _():
        y = _layer_norm(alpha * x_ref[...] + 0.5 * acc_ref[...], g_ref[...], b_ref[...])
        y32_ref[...] = y
        y16_ref[...] = y.astype(BF16)


def _ffn(x32, x16, w_in16, w_out16, l, g, b, alpha):
    M, D = x32.shape
    F = w_out16.shape[1]
    tm = _pick_tile(M, 640)
    tf = _pick_tile(F, 512, LANES)
    nf = F // tf
    return pl.pallas_call(
        functools.partial(_ffn_kernel, alpha=alpha),
        grid=(M // tm, nf),
        in_specs=[
            pl.BlockSpec((tm, D), lambda i, f: (i, 0)),
            pl.BlockSpec((tm, D), lambda i, f: (i, 0)),
            pl.BlockSpec((None, D, tf), lambda i, f: (l, 0, f)),
            pl.BlockSpec((None, D, tf), lambda i, f: (l, 0, nf + f)),
            pl.BlockSpec((None, tf, D), lambda i, f: (l, f, 0)),
            pl.BlockSpec((1, D), lambda i, f: (0, 0)),
            pl.BlockSpec((1, D), lambda i, f: (0, 0)),
        ],
        out_specs=(pl.BlockSpec((tm, D), lambda i, f: (i, 0)),
                   pl.BlockSpec((tm, D), lambda i, f: (i, 0))),
        out_shape=(jax.ShapeDtypeStruct((M, D), F32), jax.ShapeDtypeStruct((M, D), BF16)),
        scratch_shapes=[pltpu.VMEM((tm, D), F32)],
        compiler_params=_cparams(("parallel", "arbitrary")),
        name="ffn",
    )(x32, x16, w_in16, w_in16, w_out16, g, b)


def _proj_kernel(x_ref, w_ref, lb_ref, qg_ref, kvg_ref, tab_ref, pa_ref, kc_ref, vc_ref, ckr_ref, o16_ref):
    j = pl.program_id(1)
    tm = x_ref.shape[0]

    def acc():
        return _dot(x_ref[...], w_ref[...])

    def tab(k):
        return tab_ref[:, k * LANES:(k + 1) * LANES]

    def rope_heads(t0, shift):
        c, s1, s2 = tab(t0), tab(t0 + 1), tab(t0 + 2)
        a = acc()
        return [_rope128(a[:, h * LANES:(h + 1) * LANES], c, s1, s2, shift) for h in range(PROJ_TILE // LANES)]

    @pl.when((j == 0) | (j == 3) | (j == 7))
    def _():
        pa_ref[...] = _silu(acc())

    @pl.when(j == 1)
    def _():
        z = acc()
        log_sig = -(jnp.maximum(-z, 0.0) + jnp.log1p(jnp.exp(-jnp.abs(z))))
        a = lb_ref[0:1, :]
        c = lb_ref[1:2, :] + log_sig
        pa_ref[...] = jnp.maximum(a, c) + jnp.log1p(jnp.exp(-jnp.abs(a - c)))

    @pl.when((j == 2) | (j == 6))
    def _():
        pa_ref[...] = acc()

    @pl.when(j == 4)
    def _():
        for h, r in enumerate(rope_heads(0, DK_B // 2)):
            pa_ref[:, h * LANES:(h + 1) * LANES] = r

    @pl.when(j == 5)
    def _():
        for h, r in enumerate(rope_heads(0, DK_B // 2)):
            pa_ref[:, h * LANES:(h + 1) * LANES] = r * DK_B ** -0.5

    @pl.when(j == 8)
    def _():
        for h, r in enumerate(rope_heads(3, PARTIAL_ROT // 2)):
            o16_ref[:, h * LANES:(h + 1) * LANES] = r.astype(BF16)

    @pl.when(j == 9)
    def _():
        for h, r in enumerate(rope_heads(3, PARTIAL_ROT // 2)):
            kc_ref[pl.ds(h, tm, stride=H_C), :] = r
            o16_ref[:, h * LANES:(h + 1) * LANES] = r.astype(BF16)

    @pl.when(j == 10)
    def _():
        a = acc()
        for h in range(H_C):
            vc_ref[pl.ds(h, tm, stride=H_C), :] = a[:, h * LANES:(h + 1) * LANES]
        o16_ref[...] = a.astype(BF16)

    @pl.when(j == 11)
    def _():
        o16_ref[...] = _rms_norm(acc(), qg_ref[...]).astype(BF16)

    @pl.when(j == 12)
    def _():
        a = acc()
        ckv = _rms_norm(a[:, :KV_LORA], kvg_ref[...])
        kr = _rope128(a[:, KV_LORA:KV_LORA + LANES], tab(6), tab(7), tab(8), D_ROPE // 2)
        zeros = jnp.zeros((tm, PROJ_TILE - KV_LORA - LANES), F32)
        ckr_ref[:, :KV_LORA] = ckv
        ckr_ref[:, KV_LORA:KV_LORA + LANES] = kr
        ckr_ref[:, KV_LORA + LANES:] = zeros
        o16_ref[:, :KV_LORA] = ckv.astype(BF16)
        o16_ref[:, KV_LORA:KV_LORA + LANES] = kr.astype(BF16)
        o16_ref[:, KV_LORA + LANES:] = zeros.astype(BF16)


def _proj(x16, w16, l, lb2, qg, kvg, tab):
    M, D = x16.shape
    tm = _pick_tile(M, 1024)
    nj = PROJ_COLS // PROJ_TILE
    once = lambda i, j: (i, 0)
    return pl.pallas_call(
        _proj_kernel,
        grid=(M // tm, nj),
        in_specs=[
            pl.BlockSpec((tm, D), once),
            pl.BlockSpec((None, D, PROJ_TILE), lambda i, j: (l, 0, j)),
            pl.BlockSpec((2, PROJ_TILE), lambda i, j: (0, 0)),
            pl.BlockSpec((1, Q_LORA), lambda i, j: (0, 0)),
            pl.BlockSpec((1, KV_LORA), lambda i, j: (0, 0)),
            pl.BlockSpec((tm, 9 * LANES), once),
        ],
        out_specs=(pl.BlockSpec((tm, PROJ_TILE), lambda i, j: (i, jnp.minimum(j, PA_TILES - 1))),
                   pl.BlockSpec((tm * H_C, LANES), once),
                   pl.BlockSpec((tm * H_C, LANES), once),
                   pl.BlockSpec((tm, PROJ_TILE), once),
                   pl.BlockSpec((tm, PROJ_TILE), lambda i, j: (i, jnp.maximum(j - P16_FIRST, 0)))),
        out_shape=(jax.ShapeDtypeStruct((M, PA_TILES * PROJ_TILE), F32),
                   jax.ShapeDtypeStruct((M * H_C, LANES), F32),
                   jax.ShapeDtypeStruct((M * H_C, LANES), F32),
                   jax.ShapeDtypeStruct((M, PROJ_TILE), F32),
                   jax.ShapeDtypeStruct((M, P16_COLS), BF16)),
        compiler_params=_cparams(("parallel", "arbitrary")),
        name="proj",
    )(x16, w16, lb2, qg, kvg, tab)


def _q_up_kernel(x_ref, w_ref, tab_ref, o_ref):
    acc = _dot(x_ref[...], w_ref[...])
    nope = H_D * D_NOPE
    o_ref[:, :nope] = acc[:, :nope].astype(BF16)
    c, s1, s2 = (tab_ref[:, k * LANES:(k + 1) * LANES] for k in range(3))
    for h in range(H_D):
        lo = nope + h * LANES
        o_ref[:, lo:lo + LANES] = _rope128(acc[:, lo:lo + LANES], c, s1, s2, D_ROPE // 2).astype(BF16)


def _q_up(P16, w16, l, tab):
    M = P16.shape[0]
    N = w16.shape[2]
    tm = _pick_tile(M, 640)
    return pl.pallas_call(
        _q_up_kernel,
        grid=(M // tm,),
        in_specs=[
            pl.BlockSpec((tm, Q_LORA), lambda i: (i, 11 - P16_FIRST)),
            pl.BlockSpec((None, Q_LORA, N), lambda i: (l, 0, 0)),
            pl.BlockSpec((tm, 3 * LANES), lambda i: (i, 2)),
        ],
        out_specs=pl.BlockSpec((tm, N), lambda i: (i, 0)),
        out_shape=jax.ShapeDtypeStruct((M, N), BF16),
        compiler_params=_cparams(("parallel",)),
        name="q_up",
    )(P16, w16, tab)


def _kv_up_kernel(x_ref, w_ref, o_ref):
    o_ref[...] = _dot(x_ref[...].astype(BF16), w_ref[...]).astype(BF16)


def _kv_up(x, col_block, w16, l):
    M = x.shape[0]
    N = w16.shape[2]
    tm = _pick_tile(M, 1024)
    return pl.pallas_call(
        _kv_up_kernel,
        grid=(M // tm,),
        in_specs=[
            pl.BlockSpec((tm, KV_LORA), lambda i: (i, col_block)),
            pl.BlockSpec((None, KV_LORA, N), lambda i: (l, 0, 0)),
        ],
        out_specs=pl.BlockSpec((tm, N), lambda i: (i, 0)),
        out_shape=jax.ShapeDtypeStruct((M, N), BF16),
        compiler_params=_cparams(("parallel",)),
        name="kv_up",
    )(x, w16)


def _merge_kernel(h_ref, y_ref, wg_ref, wb_ref, o_ref, acc_ref):
    n = pl.program_id(2)

    @pl.when(n == 0)
    def _():
        acc_ref[...] = jnp.zeros_like(acc_ref)

    gate = jax.nn.sigmoid(_dot(h_ref[...], wg_ref[...]))
    acc_ref[...] += gate * _dot(y_ref[...], wb_ref[...])

    @pl.when(n == pl.num_programs(2) - 1)
    def _():
        o_ref[...] = acc_ref[...].astype(BF16)


def _merge(h16, y, wg16, wb16, l):
    M, D = h16.shape
    W = BRANCH_W
    tm = _pick_tile(M, 640)
    tn = 2048
    return pl.pallas_call(
        _merge_kernel,
        grid=(M // tm, D // tn, N_BRANCH),
        in_specs=[
            pl.BlockSpec((tm, D), lambda i, j, n: (i, 0)),
            pl.BlockSpec((tm, W), lambda i, j, n: (i, n)),
            pl.BlockSpec((None, None, D, tn), lambda i, j, n: (l, n, 0, j)),
            pl.BlockSpec((None, None, W, tn), lambda i, j, n: (l, n, 0, j)),
        ],
        out_specs=pl.BlockSpec((tm, tn), lambda i, j, n: (i, j)),
        out_shape=jax.ShapeDtypeStruct((M, D), BF16),
        scratch_shapes=[pltpu.VMEM((tm, tn), F32)],
        compiler_params=_cparams(("parallel", "arbitrary", "arbitrary")),
        name="merge",
    )(h16, y, wg16, wb16)


def _mix_out_kernel(x_ref, m_ref, w_ref, g_ref, b_ref, y32_ref, y16_ref, *, alpha):
    y = _layer_norm(alpha * x_ref[...] + _dot(m_ref[...], w_ref[...]), g_ref[...], b_ref[...])
    y32_ref[...] = y
    y16_ref[...] = y.astype(BF16)


def _mix_out(x32, merged16, w16, l, g, b, alpha):
    M, D = x32.shape
    tm = _pick_tile(M, 640)
    return pl.pallas_call(
        functools.partial(_mix_out_kernel, alpha=alpha),
        grid=(M // tm,),
        in_specs=[
            pl.BlockSpec((tm, D), lambda i: (i, 0)),
            pl.BlockSpec((tm, D), lambda i: (i, 0)),
            pl.BlockSpec((None, D, D), lambda i: (l, 0, 0)),
            pl.BlockSpec((1, D), lambda i: (0, 0)),
            pl.BlockSpec((1, D), lambda i: (0, 0)),
        ],
        out_specs=(pl.BlockSpec((tm, D), lambda i: (i, 0)),
                   pl.BlockSpec((tm, D), lambda i: (i, 0))),
        out_shape=(jax.ShapeDtypeStruct((M, D), F32), jax.ShapeDtypeStruct((M, D), BF16)),
        compiler_params=_cparams(("parallel",)),
        name="mix_out",
    )(x32, merged16, w16, g, b)


def _y_alias(y_prev, n_inputs):
    if y_prev is None:
        return [], [], {}
    return [pl.BlockSpec(memory_space=pl.ANY)], [y_prev], {n_inputs: 0}


def _state_spec(s0, layer, H, DK, DV):
    if layer is None:
        return pl.BlockSpec((None, H, DK, DV), lambda b, c: (b, 0, 0, 0))
    return pl.BlockSpec((None, None, H, DK, DV), lambda b, c: (layer, b, 0, 0, 0))


def _hgrn_kernel(*refs, C, has_s0):
    q_ref, lf_ref, v_ref, gate_ref = refs[:4]
    s0_ref = refs[4] if has_s0 else None
    g_ref = refs[4 + has_s0]
    y_ref, s_out_ref, st_ref = refs[-3:]
    step = pl.program_id(1)
    n_sub = q_ref.shape[0] // C
    nv = C // SUBLANES

    @pl.when(step == 0)
    def _():
        for h in range(H_A):
            st_ref[h] = s0_ref[h].T if has_s0 else jnp.zeros((DV_A, DK_A), F32)

    row8 = lax.broadcasted_iota(jnp.int32, (SUBLANES, LANES), 0)
    lane8 = lax.broadcasted_iota(jnp.int32, (SUBLANES, C), 1)
    tril = (lax.broadcasted_iota(jnp.int32, (C, C), 1) <= lax.broadcasted_iota(jnp.int32, (C, C), 0)).astype(F32)

    def chunk(ci, carry):
        rs = pl.ds(pl.multiple_of(ci * C, C), C)
        heads = range(H_A)
        sls = [slice(h * LANES, (h + 1) * LANES) for h in heads]
        b_all = jnp.dot(tril, lf_ref[rs, :], preferred_element_type=F32, precision=lax.Precision.HIGHEST)
        q = [q_ref[rs, sl] for sl in sls]
        v16 = [v_ref[rs, sl].astype(BF16) for sl in sls]
        k = [1.0 - jnp.exp(lf_ref[rs, sl]) for sl in sls]
        b = [b_all[:, sl] for sl in sls]
        st = [st_ref[h] for h in heads]
        o = [_dot_nt((q[h] * jnp.exp(b[h])).astype(BF16), st[h].astype(BF16)) for h in heads]
        res = []
        for h in heads:
            b2 = b[h] * LOG2E
            pieces = []
            for s in range(C):
                r0 = (s // SUBLANES) * SUBLANES
                d = b2[r0:] - b2[s:s + 1]
                head = jnp.where(row8 >= s % SUBLANES, d[:SUBLANES], -jnp.inf)
                d = head if C - r0 == SUBLANES else jnp.concatenate([head, d[SUBLANES:]], 0)
                pieces.append(q[h][r0:] * jnp.exp2(d))
            res.append(_dot_nt(jnp.concatenate(pieces, 0).astype(BF16), k[h].astype(BF16)))
        for h in heads:
            a_parts = [jnp.zeros((SUBLANES, C), F32) for _ in range(nv)]
            off = 0
            for s in range(C):
                for i in range(s // SUBLANES, nv):
                    a_parts[i] = a_parts[i] + jnp.where(lane8 == s, res[h][off:off + SUBLANES], 0.0)
                    off += SUBLANES
            o[h] = o[h] + _dot(jnp.concatenate(a_parts, 0).astype(BF16), v16[h])
        for h in heads:
            b_last = b[h][C - 1:C, :]
            kd = (k[h] * jnp.exp(b_last - b[h])).astype(BF16)
            st_ref[h] = st[h] * jnp.exp(b_last) + _dot_tn(v16[h], kd)
            y_ref[rs, sls[h]] = (_rms_norm(o[h], g_ref[...]) * gate_ref[rs, sls[h]]).astype(BF16)
        return carry

    lax.fori_loop(0, n_sub, chunk, 0)

    @pl.when(step == pl.num_programs(1) - 1)
    def _():
        for h in range(H_A):
            s_out_ref[h] = st_ref[h].T


def _hgrn(Pa, s0, s0_layer, g, off, B, T, M, y_prev):
    C = min(T, 64)
    rows_blk = min(T, 256)
    nb = T // rows_blk
    base = off // rows_blk
    width = H_A * DV_A

    def rows(colblk):
        return pl.BlockSpec((rows_blk, width), lambda b, c: (base + b * nb + c, colblk))

    s_specs, s_args = ([], []) if s0 is None else ([_state_spec(s0, s0_layer, H_A, DK_A, DV_A)], [s0])
    a_specs, a_args, aliases = _y_alias(y_prev, 5 + len(s_args))
    return pl.pallas_call(
        functools.partial(_hgrn_kernel, C=C, has_s0=bool(s_args)),
        grid=(B, nb),
        in_specs=[rows(0), rows(1), rows(2), rows(3)] + s_specs + [pl.BlockSpec((1, DV_A), lambda b, c: (0, 0))] + a_specs,
        out_specs=(pl.BlockSpec((rows_blk, width), lambda b, c: (base + b * nb + c, 0)),
                   pl.BlockSpec((None, H_A, DK_A, DV_A), lambda b, c: (b, 0, 0, 0))),
        out_shape=(jax.ShapeDtypeStruct((M, N_BRANCH * BRANCH_W), BF16),
                   jax.ShapeDtypeStruct((B, H_A, DK_A, DV_A), F32)),
        scratch_shapes=[pltpu.VMEM((H_A, DV_A, DK_A), F32)],
        input_output_aliases=aliases,
        compiler_params=_cparams(("parallel", "arbitrary")),
        name="hgrn",
    )(Pa, Pa, Pa, Pa, *s_args, g, *a_args)


def _ret_kernel(*refs, has_s0):
    q_ref, k_ref, v_ref, gate_ref = refs[:4]
    s0_ref = refs[4] if has_s0 else None
    g_ref = refs[4 + has_s0]
    y_ref, s_out_ref, s_ref = refs[-3:]
    c = pl.program_id(1)
    C = q_ref.shape[0]

    @pl.when(c == 0)
    def _():
        s_ref[...] = s0_ref[...] if has_s0 else jnp.zeros_like(s_ref)

    ti = lax.broadcasted_iota(jnp.int32, (C, C), 0)
    si = lax.broadcasted_iota(jnp.int32, (C, C), 1)
    dist = (ti - si).astype(F32)
    t1 = (lax.broadcasted_iota(jnp.int32, (C, 1), 0) + 1).astype(F32)

    heads = range(H_B)
    sls = [slice(h * LANES, (h + 1) * LANES) for h in heads]
    log_gamma = [math.log1p(-(2.0 ** (-5.0 - h))) for h in heads]
    q = [q_ref[:, sl] for sl in sls]
    k = [k_ref[:, sl] for sl in sls]
    v16 = [v_ref[:, sl].astype(BF16) for sl in sls]
    s = [s_ref[h] for h in heads]
    a = [_dot_nt(q[h].astype(BF16), k[h].astype(BF16))
         * jnp.exp(jnp.where(ti >= si, dist * log_gamma[h], -jnp.inf)) for h in heads]
    o = [_dot(a[h].astype(BF16), v16[h])
         + _dot((q[h] * jnp.exp(t1 * log_gamma[h])).astype(BF16), s[h].astype(BF16)) for h in heads]
    for h in heads:
        kd = (k[h] * jnp.exp((C - t1) * log_gamma[h])).astype(BF16)
        s_ref[h] = math.exp(C * log_gamma[h]) * s[h] + _dot_tn(kd, v16[h])
    for h in heads:
        mu = jnp.mean(o[h], -1, keepdims=True)
        d = o[h] - mu
        var = jnp.mean(d * d, -1, keepdims=True)
        y = d * lax.rsqrt(var + EPS) * g_ref[...] * gate_ref[:, sls[h]]
        y_ref[:, sls[h]] = y.astype(BF16)

    @pl.when(c == pl.num_programs(1) - 1)
    def _():
        s_out_ref[...] = s_ref[...]


def _ret(Pa, s0, s0_layer, g, off, B, T, M, y_prev):
    C = min(T, 128)
    nc = T // C
    base = off // C
    width = H_B * DV_B

    def rows(colblk):
        return pl.BlockSpec((C, width), lambda b, c: (base + b * nc + c, colblk))

    s_specs, s_args = ([], []) if s0 is None else ([_state_spec(s0, s0_layer, H_B, DK_B, DV_B)], [s0])
    a_specs, a_args, aliases = _y_alias(y_prev, 5 + len(s_args))
    return pl.pallas_call(
        functools.partial(_ret_kernel, has_s0=bool(s_args)),
        grid=(B, nc),
        in_specs=[rows(4), rows(5), rows(6), rows(7)] + s_specs + [pl.BlockSpec((1, DV_B), lambda b, c: (0, 0))] + a_specs,
        out_specs=(pl.BlockSpec((C, width), lambda b, c: (base + b * nc + c, 1)),
                   pl.BlockSpec((None, H_B, DK_B, DV_B), lambda b, c: (b, 0, 0, 0))),
        out_shape=(jax.ShapeDtypeStruct((M, N_BRANCH * BRANCH_W), BF16),
                   jax.ShapeDtypeStruct((B, H_B, DK_B, DV_B), F32)),
        scratch_shapes=[pltpu.VMEM((H_B, DK_B, DV_B), F32)],
        input_output_aliases=aliases,
        compiler_params=_cparams(("parallel", "arbitrary")),
        name="retention",
    )(Pa, Pa, Pa, Pa, *s_args, g, *a_args)


def _past_blocks(n, cap=512):
    out, r = [], 0
    while r < n:
        w = min(cap, n - r)
        out.append((r, w))
        r += w
    return out


def _colmax(mrun, s):
    n = s.shape[1]
    if n % LANES:
        return jnp.maximum(mrun, jnp.max(s, -1, keepdims=True))
    for gi in range(n // LANES):
        mrun = jnp.maximum(mrun, s[:, gi * LANES:(gi + 1) * LANES])
    return mrun


def _attend(n_heads, score_fn, value_fn, n_past, bq, qi, bias, sp_ref, ss_ref, m_ref, acc_ref):
    blocks = _past_blocks(n_past)
    heads = range(n_heads)
    m_ref[...] = jnp.full(m_ref.shape, NEG_BIG, F32)

    def keep(h, dst, idx, s):
        dst[idx] = s
        m_ref[h] = _colmax(m_ref[h], s)

    for r0, n in blocks:
        for h in heads:
            keep(h, sp_ref, (h, slice(None), slice(r0, r0 + n)), score_fn(h, "past", r0, n))

    def scores(j, carry):
        r0 = pl.multiple_of(j * bq, bq)
        for h in heads:
            keep(h, ss_ref, (h, j), score_fn(h, "self", r0, bq))
        return carry

    lax.fori_loop(0, qi, scores, 0)
    r_diag = pl.multiple_of(qi * bq, bq)
    for h in heads:
        s = score_fn(h, "self", r_diag, bq)
        keep(h, ss_ref, (h, qi), s if bias is None else s + bias)
    m = [jnp.max(m_ref[h], -1, keepdims=True) for h in heads]

    acc_ref[...] = jnp.zeros_like(acc_ref)

    def accumulate(h, s, v16):
        p = jnp.exp2(s - m[h]).astype(BF16)
        acc_ref[h] += _dot(p, jnp.concatenate([v16, jnp.ones_like(v16)], 1))

    for r0, n in blocks:
        for h in heads:
            accumulate(h, sp_ref[h, :, r0:r0 + n], value_fn(h, "past", r0, n))

    def weighted(j, carry):
        r0 = pl.multiple_of(j * bq, bq)
        for h in heads:
            accumulate(h, ss_ref[h, j], value_fn(h, "self", r0, bq))
        return carry

    lax.fori_loop(0, qi, weighted, 0)
    for h in heads:
        accumulate(h, ss_ref[h, qi], value_fn(h, "self", r_diag, bq))
    dv = acc_ref.shape[2] // 2
    return [acc_ref[h, :, :dv] / acc_ref[h, :, dv:] for h in heads]


def _chunk_bias(rows, bq):
    if bq <= CHUNK:
        return None
    qc = (lax.broadcasted_iota(jnp.int32, (rows, bq), 0) % bq) // CHUNK
    kc = lax.broadcasted_iota(jnp.int32, (rows, bq), 1) // CHUNK
    return jnp.where(kc <= qc, 0.0, -jnp.inf).astype(F32)


def _attn_scratch(n_heads, rows, n_past, nq, bq, dv):
    return [pltpu.VMEM((n_heads, rows, max(LANES, -(-n_past // LANES) * LANES)), F32),
            pltpu.VMEM((n_heads, nq, rows, bq), F32),
            pltpu.VMEM((n_heads, rows, LANES), F32),
            pltpu.VMEM((n_heads, rows, 2 * dv), F32)]


def _diff_kernel(*refs, past_mode, n_past, lam_init):
    refs = list(refs)
    sp_ref, ss_ref, m_ref, acc_ref = refs[-4:]
    y_ref = refs[-5]
    if past_mode:
        q_ref, ks_ref, vs_ref, kp_ref, vp_ref, lam_ref, g_ref = refs[:7]
    else:
        q_ref, ks_ref, vs_ref, lam_ref, g_ref = refs[:5]
        kp_ref = vp_ref = None
    qi = pl.program_id(1)
    bq = q_ref.shape[0]
    lam_p = lam_ref[...]
    lam = (jnp.exp(jnp.sum(lam_p[0:1] * lam_p[1:2], -1, keepdims=True))
           - jnp.exp(jnp.sum(lam_p[2:3] * lam_p[3:4], -1, keepdims=True)) + lam_init)
    lane = lax.broadcasted_iota(jnp.int32, (bq, LANES), 1)
    bias = _chunk_bias(2 * bq, bq)
    scale = DH_C ** -0.5 * LOG2E
    sls = [slice(h * LANES, (h + 1) * LANES) for h in range(H_C)]
    q2 = []
    for sl in sls:
        qh = q_ref[:, sl]
        zero = jnp.zeros_like(qh)
        q2.append(jnp.concatenate([jnp.where(lane < DH_C, qh, zero), jnp.where(lane >= DH_C, qh, zero)], 0))

    def rows_of(ref, h, src, r0, n):
        if src == "past" and past_mode == "cache":
            return ref[pl.ds(H_C * r0 + h, n, stride=H_C), :]
        return ref[pl.ds(r0, n), sls[h]]

    def score_fn(h, src, r0, n):
        k = rows_of(kp_ref if src == "past" else ks_ref, h, src, r0, n)
        return _dot_nt(q2[h], k.astype(BF16)) * scale

    def value_fn(h, src, r0, n):
        return rows_of(vp_ref if src == "past" else vs_ref, h, src, r0, n).astype(BF16)

    a = _attend(H_C, score_fn, value_fn, n_past, bq, qi, bias, sp_ref, ss_ref, m_ref, acc_ref)
    for h, sl in enumerate(sls):
        d = a[h][:bq] - lam * a[h][bq:]
        y_ref[:, sl] = (_rms_norm(d, g_ref[...]) * (1.0 - lam_init)).astype(BF16)


def _q_block(T):
    for bq in (256, 128):
        if T % bq == 0:
            return bq
    return T


def _diff_attn(P16, past, lam_p, g, off, B, T, M, lam_init, y_prev):
    bq = _q_block(T)
    nq = T // bq
    width = H_C * DV_C
    in_specs = [
        pl.BlockSpec((bq, width), lambda b, i: (off // bq + b * nq + i, 0)),
        pl.BlockSpec((T, width), lambda b, i: (off // T + b, 1)),
        pl.BlockSpec((T, width), lambda b, i: (off // T + b, 2)),
    ]
    args = [P16, P16, P16]
    n_past, past_mode = 0, None
    if past is not None:
        past_mode = past[0]
        if past_mode == "flat":
            _, poff, n_past = past
            in_specs += [pl.BlockSpec((n_past, width), lambda b, i: (poff // n_past + b, 1)),
                         pl.BlockSpec((n_past, width), lambda b, i: (poff // n_past + b, 2))]
            args += [P16, P16]
        else:
            _, pk, pv, layer = past
            n_past = pk.shape[2] // H_C
            spec = pl.BlockSpec((None, None, n_past * H_C, LANES), lambda b, i: (layer, b, 0, 0))
            in_specs += [spec, spec]
            args += [pk, pv]
    in_specs += [pl.BlockSpec((4, DH_C), lambda b, i: (0, 0)), pl.BlockSpec((1, DV_C), lambda b, i: (0, 0))]
    args += [lam_p, g]
    a_specs, a_args, aliases = _y_alias(y_prev, len(args))
    return pl.pallas_call(
        functools.partial(_diff_kernel, past_mode=past_mode, n_past=n_past, lam_init=lam_init),
        grid=(B, nq),
        in_specs=in_specs + a_specs,
        out_specs=pl.BlockSpec((bq, width), lambda b, i: (off // bq + b * nq + i, 2)),
        out_shape=jax.ShapeDtypeStruct((M, N_BRANCH * BRANCH_W), BF16),
        scratch_shapes=_attn_scratch(H_C, 2 * bq, n_past, nq, bq, DV_C),
        input_output_aliases=aliases,
        compiler_params=_cparams(("parallel", "arbitrary")),
        name="diff_attn",
    )(*args, *a_args)


def _mla_kernel(*refs, past_mode, n_past):
    refs = list(refs)
    qi = pl.program_id(1)
    if past_mode == "cache":
        sp_ref, ss_ref, m_ref, acc_ref, kvp_ref = refs[-5:]
        y_ref = refs[-6]
        q_ref, kvs_ref, krs_ref, ckvp_ref, krp_ref, wkv_ref = refs[:6]

        @pl.when(qi == 0)
        def _():
            for r0, n in _past_blocks(n_past):
                kvp_ref[r0:r0 + n, :] = _dot(ckvp_ref[r0:r0 + n, :].astype(BF16), wkv_ref[...]).astype(BF16)
    else:
        sp_ref, ss_ref, m_ref, acc_ref = refs[-4:]
        y_ref = refs[-5]
        if past_mode:
            q_ref, kvs_ref, krs_ref, kvp_ref, krp_ref = refs[:5]
        else:
            q_ref, kvs_ref, krs_ref = refs[:3]
            kvp_ref = krp_ref = None
    bq = q_ref.shape[0]
    bias = _chunk_bias(bq, bq)
    scale = (D_NOPE + D_ROPE) ** -0.5 * LOG2E
    qn = [q_ref[:, h * D_NOPE:(h + 1) * D_NOPE] for h in range(H_D)]
    qr = [q_ref[:, H_D * D_NOPE + h * LANES:H_D * D_NOPE + h * LANES + D_ROPE] for h in range(H_D)]

    def score_fn(h, src, r0, n):
        kv = kvp_ref if src == "past" else kvs_ref
        kr = krp_ref if src == "past" else krs_ref
        ksl = slice(h * (D_NOPE + DV_D), h * (D_NOPE + DV_D) + D_NOPE)
        s = _dot_nt(qn[h], kv[pl.ds(r0, n), ksl]) + _dot_nt(qr[h], kr[pl.ds(r0, n), :D_ROPE].astype(BF16))
        return s * scale

    def value_fn(h, src, r0, n):
        vsl = slice(h * (D_NOPE + DV_D) + D_NOPE, (h + 1) * (D_NOPE + DV_D))
        return (kvp_ref if src == "past" else kvs_ref)[pl.ds(r0, n), vsl]

    a = _attend(H_D, score_fn, value_fn, n_past, bq, qi, bias, sp_ref, ss_ref, m_ref, acc_ref)
    for h in range(H_D):
        y_ref[:, h * DV_D:(h + 1) * DV_D] = a[h].astype(BF16)


def _mla_attn(q16, kv16, P16, past, off, B, T, M, y_prev):
    bq = _q_block(T)
    nq = T // bq
    wq = q16.shape[1]
    wkv = kv16.shape[1]
    kr_blk = ((12 - P16_FIRST) * PROJ_TILE + KV_LORA) // LANES
    in_specs = [
        pl.BlockSpec((bq, wq), lambda b, i: (off // bq + b * nq + i, 0)),
        pl.BlockSpec((T, wkv), lambda b, i: (off // T + b, 0)),
        pl.BlockSpec((T, LANES), lambda b, i: (off // T + b, kr_blk)),
    ]
    args = [q16, kv16, P16]
    n_past, past_mode = 0, None
    scratch = []
    if past is not None:
        past_mode = past[0]
        if past_mode == "flat":
            _, poff, n_past = past
            in_specs += [pl.BlockSpec((n_past, wkv), lambda b, i: (poff // n_past + b, 0)),
                         pl.BlockSpec((n_past, LANES), lambda b, i: (poff // n_past + b, kr_blk))]
            args += [kv16, P16]
        else:
            _, pckv, pkr, w_ukv16, layer = past
            n_past = pckv.shape[2]
            in_specs += [pl.BlockSpec((None, None, n_past, KV_LORA), lambda b, i: (layer, b, 0, 0)),
                         pl.BlockSpec((None, None, n_past, D_ROPE), lambda b, i: (layer, b, 0, 0)),
                         pl.BlockSpec((None, KV_LORA, wkv), lambda b, i: (layer, 0, 0))]
            args += [pckv, pkr, w_ukv16]
            scratch = [pltpu.VMEM((n_past, wkv), BF16)]
    width = H_D * DV_D
    a_specs, a_args, aliases = _y_alias(y_prev, len(args))
    return pl.pallas_call(
        functools.partial(_mla_kernel, past_mode=past_mode, n_past=n_past),
        grid=(B, nq),
        in_specs=in_specs + a_specs,
        out_specs=pl.BlockSpec((bq, width), lambda b, i: (off // bq + b * nq + i, 3)),
        out_shape=jax.ShapeDtypeStruct((M, N_BRANCH * BRANCH_W), BF16),
        scratch_shapes=_attn_scratch(H_D, bq, n_past, nq, bq, DV_D) + scratch,
        input_output_aliases=aliases,
        compiler_params=_cparams(("parallel", "arbitrary")),
        name="mla_attn",
    )(*args, *a_args)


def _rope_table(pos, period, half, rot_dim, theta):
    lane = np.arange(LANES)
    li = lane % period
    first = li < half
    second = (li >= half) & (li < rot_dim)
    idx = np.where(first, li, np.where(second, li - half, 0))
    freq = jnp.power(jnp.float32(theta), -jnp.arange(half, dtype=F32) / half)[idx]
    ang = pos.astype(F32)[:, None] * freq[None, :]
    cos, sin = jnp.cos(ang), jnp.sin(ang)
    rot = jnp.asarray(first | second)[None, :]
    return [jnp.where(rot, cos, 1.0), jnp.where(jnp.asarray(second)[None, :], sin, 0.0),
            jnp.where(jnp.asarray(first)[None, :], -sin, 0.0)]


def _rope_tables(pos):
    tabs = (_rope_table(pos, LANES, DK_B // 2, DK_B, RET_THETA)
            + _rope_table(pos, DH_C, PARTIAL_ROT // 2, PARTIAL_ROT, ROPE_THETA)
            + _rope_table(pos, LANES, D_ROPE // 2, D_ROPE, ROPE_THETA))
    return jnp.concatenate(tabs, axis=1)


def _uq_layout(w_uq):
    depth = w_uq.shape[0]
    w = w_uq.reshape(depth, Q_LORA, H_D, D_NOPE + D_ROPE)
    nope = w[..., :D_NOPE].reshape(depth, Q_LORA, H_D * D_NOPE)
    rope = jnp.pad(w[..., D_NOPE:], ((0, 0), (0, 0), (0, 0), (0, LANES - D_ROPE))).reshape(depth, Q_LORA, H_D * LANES)
    return jnp.concatenate([nope, rope], axis=2).astype(BF16)


def kernel(x_prompt, x_sample, cache_diff_k, cache_diff_v, cache_mla_ckv, cache_mla_krope, state_hgrn, state_ret, meta_tokens, w_ffn1_in, w_ffn1_out, ln1_g, ln1_b, w_in, lb_logits, hgrn_norm_g, ret_norm_g, diff_lambda_q1, diff_lambda_k1, diff_lambda_q2, diff_lambda_k2, diff_norm_g, mla_q_norm_g, mla_kv_norm_g, w_mla_uq, w_mla_ukv, w_branch, w_merge_gate, w_mix_out, ln2_g, ln2_b, w_ffn2_in, w_ffn2_out, ln3_g, ln3_b):
    depth = w_in.shape[0]
    Bp, S, D = x_prompt.shape
    Bs, Ts, _ = x_sample.shape
    n_cache = cache_diff_k.shape[2]
    past = n_cache - N_META
    assert S % (2 * CHUNK) == 0 and Ts <= CHUNK and past % CHUNK == 0 and (past + Ts - 1) // CHUNK == past // CHUNK
    alpha = (2 * depth) ** 0.25

    off_s = Bp * S
    off_m = off_s + Bs * Ts
    M = off_m + Bp * N_META
    assert off_s % Ts == 0 and off_m % N_META == 0
    x = jnp.concatenate([
        x_prompt.reshape(Bp * S, D), x_sample.reshape(Bs * Ts, D),
        jnp.broadcast_to(meta_tokens[None].astype(x_prompt.dtype), (Bp, N_META, D)).reshape(Bp * N_META, D)], 0)
    x = x.astype(F32)
    x16 = x.astype(BF16)
    tab_pos = _rope_tables(jnp.arange(N_META + max(S, past + Ts), dtype=jnp.int32))
    tab = jnp.concatenate([
        jnp.tile(tab_pos[N_META:N_META + S], (Bp, 1)),
        jnp.tile(tab_pos[N_META + past:N_META + past + Ts], (Bs, 1)),
        jnp.tile(tab_pos[:N_META], (Bp, 1))], 0)

    w1i, w1o = w_ffn1_in.astype(BF16), w_ffn1_out.astype(BF16)
    w2i, w2o = w_ffn2_in.astype(BF16), w_ffn2_out.astype(BF16)
    w_in16 = jnp.pad(w_in, ((0, 0), (0, 0), (0, PROJ_COLS - w_in.shape[2]))).astype(BF16)
    w_uq16 = _uq_layout(w_mla_uq)
    w_ukv16 = w_mla_ukv.astype(BF16)
    w_gate16, w_branch16, w_out16 = w_merge_gate.astype(BF16), w_branch.astype(BF16), w_mix_out.astype(BF16)

    la, lc = _lower_bounds(lb_logits)
    groups = (("meta", off_m, Bp, N_META), ("prompt", 0, Bp, S), ("sample", off_s, Bs, Ts))
    row = lambda a: a.reshape(1, -1).astype(F32)
    out_dt = x_prompt.dtype

    ck_all = cache_diff_k.reshape(depth, Bs, n_cache * H_C, 2 * DH_C)
    cv_all = cache_diff_v.reshape(depth, Bs, n_cache * H_C, DV_C)
    sa_all, sb_all = state_hgrn.astype(F32), state_ret.astype(F32)

    def alloc(B, T, tail):
        return jnp.zeros((depth, B, T) + tail, out_dt)

    tails = {"k": (H_C, 2 * DH_C), "v": (H_C, DV_C), "ckv": (KV_LORA,), "kr": (D_ROPE,)}
    p_out = {k: alloc(Bp, N_META + S, t) for k, t in tails.items()}
    s_out = {k: alloc(Bs, Ts, t) for k, t in tails.items()}
    states = {k: [] for k in ("psa", "psb", "ssa", "ssb")}

    for l in range(depth):
        x, x16 = _ffn(x, x16, w1i, w1o, l, row(ln1_g[l]), row(ln1_b[l]), alpha)

        lb2 = jnp.stack([la[l], lc[l]], 0)
        Pa, kc4, vc4, ckr, P16 = _proj(x16, w_in16, l, lb2, row(mla_q_norm_g[l]), row(mla_kv_norm_g[l]), tab)
        q_d = _q_up(P16, w_uq16, l, tab)
        kv_new = _kv_up(P16, ((12 - P16_FIRST) * PROJ_TILE) // KV_LORA, w_ukv16, l)
        lam_p = jnp.stack([diff_lambda_q1[l], diff_lambda_k1[l], diff_lambda_q2[l], diff_lambda_k2[l]], 0).astype(F32)
        lam_init = 0.8 - 0.6 * math.exp(-0.3 * l)

        y = None
        sa_meta = sb_meta = None
        for name, off, B, T in groups:
            if name == "meta":
                sa0 = sb0 = None
                s_layer = None
                past_c = past_d = None
            elif name == "prompt":
                sa0, sb0, s_layer = sa_meta, sb_meta, None
                past_c = past_d = ("flat", off_m, N_META)
            else:
                sa0, sb0, s_layer = sa_all, sb_all, l
                past_c = ("cache", ck_all, cv_all, l)
                past_d = ("cache", cache_mla_ckv, cache_mla_krope, w_ukv16, l)
            y, sa = _hgrn(Pa, sa0, s_layer, row(hgrn_norm_g[l]), off, B, T, M, y)
            y, sb = _ret(Pa, sb0, s_layer, row(ret_norm_g[l]), off, B, T, M, y)
            y = _diff_attn(P16, past_c, lam_p, row(diff_norm_g[l]), off, B, T, M, lam_init, y)
            y = _mla_attn(q_d, kv_new, P16, past_d, off, B, T, M, y)
            if name == "meta":
                sa_meta, sb_meta = sa, sb
            elif name == "prompt":
                states["psa"].append(sa); states["psb"].append(sb)
            else:
                states["ssa"].append(sa); states["ssb"].append(sb)

        merged = _merge(x16, y, w_gate16, w_branch16, l)
        x, x16 = _mix_out(x, merged, w_out16, l, row(ln2_g[l]), row(ln2_b[l]), alpha)
        x, x16 = _ffn(x, x16, w2i, w2o, l, row(ln3_g[l]), row(ln3_b[l]), alpha)

        pieces = {"k": kc4.reshape(M, H_C, 2 * DH_C), "v": vc4.reshape(M, H_C, DV_C),
                  "ckv": ckr[:, :KV_LORA], "kr": ckr[:, KV_LORA:KV_LORA + D_ROPE]}
        for key, piece in pieces.items():
            tail = tails[key]
            p_out[key] = p_out[key].at[l, :, :N_META].set(piece[off_m:].reshape((Bp, N_META) + tail))
            p_out[key] = p_out[key].at[l, :, N_META:].set(piece[:off_s].reshape((Bp, S) + tail))
            s_out[key] = s_out[key].at[l].set(piece[off_s:off_m].reshape((Bs, Ts) + tail))

    y_prompt = x[:off_s].reshape(Bp, S, D)
    y_sample = x[off_s:off_m].reshape(Bs, Ts, D)
    st = lambda k: jnp.stack(states[k], 0).astype(out_dt)
    return (y_prompt, y_sample, p_out["k"], p_out["v"], p_out["ckv"], p_out["kr"], st("psa"), st("psb"),
            s_out["k"], s_out["v"], s_out["ckv"], s_out["kr"], st("ssa"), st("ssb"))
```

```python
import functools
import math

import numpy as np
import jax
import jax.numpy as jnp
from jax import lax
from jax.experimental import pallas as pl
from jax.experimental.pallas import tpu as pltpu

F32 = jnp.float32
BF16 = jnp.bfloat16

D_MODEL = 2048
CHUNK = 64
N_META = 16
N_BRANCH = 4
BRANCH_W = 512
H_A, DK_A, DV_A = 4, 128, 128
H_B, DK_B, DV_B = 4, 128, 128
RET_THETA = 10000.0
H_C, DH_C, DV_C = 4, 64, 128
PARTIAL_ROT = DH_C // 4
H_D = 4
Q_LORA, KV_LORA = 512, 256
D_NOPE, D_ROPE, DV_D = 128, 64, 128
D_FF = 5632
ROPE_THETA = 500000.0
EPS = 1e-5

LANES = 128
SUBLANES = 8
PROJ_TILE = 512
PROJ_COLS = 13 * PROJ_TILE
P16_FIRST = 8
P16_COLS = PROJ_COLS - P16_FIRST * PROJ_TILE
PA_TILES = 8
VMEM_LIMIT = 56 * 1024 * 1024
NEG_BIG = -1e30
LOG2E = math.log2(math.e)


def _cparams(sem):
    return pltpu.CompilerParams(dimension_semantics=sem, vmem_limit_bytes=VMEM_LIMIT)


def _pick_tile(n, cap, mult=16):
    best = None
    for t in range(mult, min(n, cap) + 1, mult):
        if n % t == 0:
            best = t
    return best if best is not None else n


def _dot(a, b):
    return jnp.dot(a, b, preferred_element_type=F32)


def _dot_nt(a, b):
    return lax.dot_general(a, b, (((1,), (1,)), ((), ())), preferred_element_type=F32)


def _dot_tn(a, b):
    return lax.dot_general(a, b, (((0,), (0,)), ((), ())), preferred_element_type=F32)


def _layer_norm(z, g, b):
    mu = jnp.mean(z, -1, keepdims=True)
    d = z - mu
    var = jnp.mean(d * d, -1, keepdims=True)
    return d * lax.rsqrt(var + EPS) * g + b


def _rms_norm(z, g):
    return z * lax.rsqrt(jnp.mean(z * z, -1, keepdims=True) + EPS) * g


def _silu(a):
    return a * jax.nn.sigmoid(a)


def _rope128(x, c, s1, s2, shift):
    return x * c + pltpu.roll(x, shift, 1) * s1 + pltpu.roll(x, LANES - shift, 1) * s2


def _lower_bound_kernel(logit_ref, la_ref, lc_ref):
    z = logit_ref[...]
    depth = z.shape[0]
    m = z[0:1]
    for l in range(1, depth):
        m = jnp.maximum(m, z[l:l + 1])
    e = jnp.exp(z - m)
    tot = e[0:1]
    for l in range(1, depth):
        tot = tot + e[l:l + 1]
    p = e / tot
    run = jnp.zeros_like(m)
    for l in range(depth):
        la_ref[l:l + 1, :] = jnp.log(run)
        lc_ref[l:l + 1, :] = jnp.log1p(-run)
        run = run + p[l:l + 1]


def _lower_bounds(lb_logits):
    shp = jax.ShapeDtypeStruct(lb_logits.shape, F32)
    return pl.pallas_call(_lower_bound_kernel, out_shape=(shp, shp), name="lower_bounds")(
        lb_logits.astype(F32))


def _ffn_kernel(x_ref, x16_ref, wa_ref, wb_ref, wo_ref, g_ref, b_ref, y32_ref, y16_ref, acc_ref, *, alpha):
    f = pl.program_id(1)

    @pl.when(f == 0)
    def _():
        acc_ref[...] = jnp.zeros_like(acc_ref)

    x = x16_ref[...]
    a = _dot(x, wa_ref[...])
    b = _dot(x, wb_ref[...])
    h = (_silu(a) * b).astype(BF16)
    acc_ref[...] += _dot(h, wo_ref[...])

    @pl.when(f == pl.num_programs(1) - 1)
    def _():
        y = _layer_norm(alpha * x_ref[...] + 0.5 * acc_ref[...], g_ref[...], b_ref[...])
        y32_ref[...] = y
        y16_ref[...] = y.astype(BF16)


def _ffn(x32, x16, w_in16, w_out16, l, g, b, alpha):
    M, D = x32.shape
    F = w_out16.shape[1]
    tm = _pick_tile(M, 640)
    tf = _pick_tile(F, 512, LANES)
    nf = F // tf
    return pl.pallas_call(
        functools.partial(_ffn_kernel, alpha=alpha),
        grid=(M // tm, nf),
        in_specs=[
            pl.BlockSpec((tm, D), lambda i, f: (i, 0)),
            pl.BlockSpec((tm, D), lambda i, f: (i, 0)),
            pl.BlockSpec((None, D, tf), lambda i, f: (l, 0, f)),
            pl.BlockSpec((None, D, tf), lambda i, f: (l, 0, nf + f)),
            pl.BlockSpec((None, tf, D), lambda i, f: (l, f, 0)),
            pl.BlockSpec((1, D), lambda i, f: (0, 0)),
            pl.BlockSpec((1, D), lambda i, f: (0, 0)),
        ],
        out_specs=(pl.BlockSpec((tm, D), lambda i, f: (i, 0)),
                   pl.BlockSpec((tm, D), lambda i, f: (i, 0))),
        out_shape=(jax.ShapeDtypeStruct((M, D), F32), jax.ShapeDtypeStruct((M, D), BF16)),
        scratch_shapes=[pltpu.VMEM((tm, D), F32)],
        compiler_params=_cparams(("parallel", "arbitrary")),
        name="ffn",
    )(x32, x16, w_in16, w_in16, w_out16, g, b)


def _proj_kernel(x_ref, w_ref, lb_ref, qg_ref, kvg_ref, tab_ref, wuq_ref, wukv_ref,
                 pa_ref, kc_ref, vc_ref, ckr_ref, o16_ref, qd_ref, kvn_ref):
    j = pl.program_id(1)
    tm = x_ref.shape[0]

    def acc():
        return _dot(x_ref[...], w_ref[...])

    def tab(k):
        return tab_ref[:, k * LANES:(k + 1) * LANES]

    def rope_heads(t0, shift):
        c, s1, s2 = tab(t0), tab(t0 + 1), tab(t0 + 2)
        a = acc()
        return [_rope128(a[:, h * LANES:(h + 1) * LANES], c, s1, s2, shift) for h in range(PROJ_TILE // LANES)]

    @pl.when((j == 0) | (j == 3) | (j == 7))
    def _():
        pa_ref[...] = _silu(acc())

    @pl.when(j == 1)
    def _():
        z = acc()
        log_sig = -(jnp.maximum(-z, 0.0) + jnp.log1p(jnp.exp(-jnp.abs(z))))
        a = lb_ref[0:1, :]
        c = lb_ref[1:2, :] + log_sig
        pa_ref[...] = jnp.maximum(a, c) + jnp.log1p(jnp.exp(-jnp.abs(a - c)))

    @pl.when((j == 2) | (j == 6))
    def _():
        pa_ref[...] = acc()

    @pl.when(j == 4)
    def _():
        for h, r in enumerate(rope_heads(0, DK_B // 2)):
            pa_ref[:, h * LANES:(h + 1) * LANES] = r

    @pl.when(j == 5)
    def _():
        for h, r in enumerate(rope_heads(0, DK_B // 2)):
            pa_ref[:, h * LANES:(h + 1) * LANES] = r * DK_B ** -0.5

    @pl.when(j == 8)
    def _():
        for h, r in enumerate(rope_heads(3, PARTIAL_ROT // 2)):
            o16_ref[:, h * LANES:(h + 1) * LANES] = r.astype(BF16)

    @pl.when(j == 9)
    def _():
        for h, r in enumerate(rope_heads(3, PARTIAL_ROT // 2)):
            kc_ref[pl.ds(h, tm, stride=H_C), :] = r
            o16_ref[:, h * LANES:(h + 1) * LANES] = r.astype(BF16)

    @pl.when(j == 10)
    def _():
        a = acc()
        for h in range(H_C):
            vc_ref[pl.ds(h, tm, stride=H_C), :] = a[:, h * LANES:(h + 1) * LANES]
        o16_ref[...] = a.astype(BF16)

    @pl.when(j == 11)
    def _():
        cq = _rms_norm(acc(), qg_ref[...]).astype(BF16)
        o16_ref[...] = cq
        q = _dot(cq, wuq_ref[...])
        nope = H_D * D_NOPE
        qd_ref[:, :nope] = q[:, :nope].astype(BF16)
        for h in range(H_D):
            lo = nope + h * LANES
            qd_ref[:, lo:lo + LANES] = _rope128(q[:, lo:lo + LANES], tab(6), tab(7), tab(8), D_ROPE // 2).astype(BF16)

    @pl.when(j == 12)
    def _():
        a = acc()
        ckv = _rms_norm(a[:, :KV_LORA], kvg_ref[...])
        kvn_ref[...] = _dot(ckv.astype(BF16), wukv_ref[...]).astype(BF16)
        kr = _rope128(a[:, KV_LORA:KV_LORA + LANES], tab(6), tab(7), tab(8), D_ROPE // 2)
        zeros = jnp.zeros((tm, PROJ_TILE - KV_LORA - LANES), F32)
        ckr_ref[:, :KV_LORA] = ckv
        ckr_ref[:, KV_LORA:KV_LORA + LANES] = kr
        ckr_ref[:, KV_LORA + LANES:] = zeros
        o16_ref[:, :KV_LORA] = ckv.astype(BF16)
        o16_ref[:, KV_LORA:KV_LORA + LANES] = kr.astype(BF16)
        o16_ref[:, KV_LORA + LANES:] = zeros.astype(BF16)


def _proj(x16, w16, l, lb2, qg, kvg, tab, w_uq16, w_ukv16):
    M, D = x16.shape
    tm = _pick_tile(M, 1024)
    nj = PROJ_COLS // PROJ_TILE
    once = lambda i, j: (i, 0)
    n_q, n_kv = w_uq16.shape[2], w_ukv16.shape[2]
    return pl.pallas_call(
        _proj_kernel,
        grid=(M // tm, nj),
        in_specs=[
            pl.BlockSpec((tm, D), once),
            pl.BlockSpec((None, D, PROJ_TILE), lambda i, j: (l, 0, j)),
            pl.BlockSpec((2, PROJ_TILE), lambda i, j: (0, 0)),
            pl.BlockSpec((1, Q_LORA), lambda i, j: (0, 0)),
            pl.BlockSpec((1, KV_LORA), lambda i, j: (0, 0)),
            pl.BlockSpec((tm, 9 * LANES), once),
            pl.BlockSpec((None, Q_LORA, n_q), lambda i, j: (l, 0, 0)),
            pl.BlockSpec((None, KV_LORA, n_kv), lambda i, j: (l, 0, 0)),
        ],
        out_specs=(pl.BlockSpec((tm, PROJ_TILE), lambda i, j: (i, jnp.minimum(j, PA_TILES - 1))),
                   pl.BlockSpec((tm * H_C, LANES), once),
                   pl.BlockSpec((tm * H_C, LANES), once),
                   pl.BlockSpec((tm, PROJ_TILE), once),
                   pl.BlockSpec((tm, PROJ_TILE), lambda i, j: (i, jnp.maximum(j - P16_FIRST, 0))),
                   pl.BlockSpec((tm, n_q), once),
                   pl.BlockSpec((tm, n_kv), once)),
        out_shape=(jax.ShapeDtypeStruct((M, PA_TILES * PROJ_TILE), F32),
                   jax.ShapeDtypeStruct((M * H_C, LANES), F32),
                   jax.ShapeDtypeStruct((M * H_C, LANES), F32),
                   jax.ShapeDtypeStruct((M, PROJ_TILE), F32),
                   jax.ShapeDtypeStruct((M, P16_COLS), BF16),
                   jax.ShapeDtypeStruct((M, n_q), BF16),
                   jax.ShapeDtypeStruct((M, n_kv), BF16)),
        compiler_params=_cparams(("parallel", "arbitrary")),
        name="proj",
    )(x16, w16, lb2, qg, kvg, tab, w_uq16, w_ukv16)


def _merge_kernel(h_ref, y_ref, wg_ref, wb_ref, o_ref, acc_ref):
    n = pl.program_id(2)

    @pl.when(n == 0)
    def _():
        acc_ref[...] = jnp.zeros_like(acc_ref)

    gate = jax.nn.sigmoid(_dot(h_ref[...], wg_ref[...]))
    acc_ref[...] += gate * _dot(y_ref[...], wb_ref[...])

    @pl.when(n == pl.num_programs(2) - 1)
    def _():
        o_ref[...] = acc_ref[...].astype(BF16)


def _merge(h16, y, wg16, wb16, l):
    M, D = h16.shape
    W = BRANCH_W
    tm = _pick_tile(M, 640)
    tn = 2048
    return pl.pallas_call(
        _merge_kernel,
        grid=(M // tm, D // tn, N_BRANCH),
        in_specs=[
            pl.BlockSpec((tm, D), lambda i, j, n: (i, 0)),
            pl.BlockSpec((tm, W), lambda i, j, n: (i, n)),
            pl.BlockSpec((None, None, D, tn), lambda i, j, n: (l, n, 0, j)),
            pl.BlockSpec((None, None, W, tn), lambda i, j, n: (l, n, 0, j)),
        ],
        out_specs=pl.BlockSpec((tm, tn), lambda i, j, n: (i, j)),
        out_shape=jax.ShapeDtypeStruct((M, D), BF16),
        scratch_shapes=[pltpu.VMEM((tm, tn), F32)],
        compiler_params=_cparams(("parallel", "arbitrary", "arbitrary")),
        name="merge",
    )(h16, y, wg16, wb16)


def _mix_out_kernel(x_ref, m_ref, w_ref, g_ref, b_ref, y32_ref, y16_ref, *, alpha):
    y = _layer_norm(alpha * x_ref[...] + _dot(m_ref[...], w_ref[...]), g_ref[...], b_ref[...])
    y32_ref[...] = y
    y16_ref[...] = y.astype(BF16)


def _mix_out(x32, merged16, w16, l, g, b, alpha):
    M, D = x32.shape
    tm = _pick_tile(M, 640)
    return pl.pallas_call(
        functools.partial(_mix_out_kernel, alpha=alpha),
        grid=(M // tm,),
        in_specs=[
            pl.BlockSpec((tm, D), lambda i: (i, 0)),
            pl.BlockSpec((tm, D), lambda i: (i, 0)),
            pl.BlockSpec((None, D, D), lambda i: (l, 0, 0)),
            pl.BlockSpec((1, D), lambda i: (0, 0)),
            pl.BlockSpec((1, D), lambda i: (0, 0)),
        ],
        out_specs=(pl.BlockSpec((tm, D), lambda i: (i, 0)),
                   pl.BlockSpec((tm, D), lambda i: (i, 0))),
        out_shape=(jax.ShapeDtypeStruct((M, D), F32), jax.ShapeDtypeStruct((M, D), BF16)),
        compiler_params=_cparams(("parallel",)),
        name="mix_out",
    )(x32, merged16, w16, g, b)


def _y_alias(y_prev, n_inputs):
    if y_prev is None:
        return [], [], {}
    return [pl.BlockSpec(memory_space=pl.ANY)], [y_prev], {n_inputs: 0}


def _state_spec(s0, layer, H, DK, DV):
    if layer is None:
        return pl.BlockSpec((None, H, DK, DV), lambda b, c: (b, 0, 0, 0))
    return pl.BlockSpec((None, None, H, DK, DV), lambda b, c: (layer, b, 0, 0, 0))


def _hgrn_kernel(*refs, C, has_s0):
    q_ref, lf_ref, v_ref, gate_ref = refs[:4]
    s0_ref = refs[4] if has_s0 else None
    g_ref = refs[4 + has_s0]
    y_ref, s_out_ref, st_ref = refs[-3:]
    step = pl.program_id(1)
    n_sub = q_ref.shape[0] // C
    nv = C // SUBLANES

    @pl.when(step == 0)
    def _():
        for h in range(H_A):
            st_ref[h] = s0_ref[h].T if has_s0 else jnp.zeros((DV_A, DK_A), F32)

    row8 = lax.broadcasted_iota(jnp.int32, (SUBLANES, LANES), 0)
    lane8 = lax.broadcasted_iota(jnp.int32, (SUBLANES, C), 1)
    tril = (lax.broadcasted_iota(jnp.int32, (C, C), 1) <= lax.broadcasted_iota(jnp.int32, (C, C), 0)).astype(F32)

    def chunk(ci, carry):
        rs = pl.ds(pl.multiple_of(ci * C, C), C)
        heads = range(H_A)
        sls = [slice(h * LANES, (h + 1) * LANES) for h in heads]
        b_all = jnp.dot(tril, lf_ref[rs, :], preferred_element_type=F32, precision=lax.Precision.HIGHEST)
        q = [q_ref[rs, sl] for sl in sls]
        v16 = [v_ref[rs, sl].astype(BF16) for sl in sls]
        k = [1.0 - jnp.exp(lf_ref[rs, sl]) for sl in sls]
        b = [b_all[:, sl] for sl in sls]
        st = [st_ref[h] for h in heads]
        o = [_dot_nt((q[h] * jnp.exp(b[h])).astype(BF16), st[h].astype(BF16)) for h in heads]
        res = []
        for h in heads:
            b2 = b[h] * LOG2E
            pieces = []
            for s in range(C):
                r0 = (s // SUBLANES) * SUBLANES
                d = b2[r0:] - b2[s:s + 1]
                head = jnp.where(row8 >= s % SUBLANES, d[:SUBLANES], -jnp.inf)
                d = head if C - r0 == SUBLANES else jnp.concatenate([head, d[SUBLANES:]], 0)
                pieces.append(q[h][r0:] * jnp.exp2(d))
            res.append(_dot_nt(jnp.concatenate(pieces, 0).astype(BF16), k[h].astype(BF16)))
        for h in heads:
            a_parts = [jnp.zeros((SUBLANES, C), F32) for _ in range(nv)]
            off = 0
            for s in range(C):
                for i in range(s // SUBLANES, nv):
                    a_parts[i] = a_parts[i] + jnp.where(lane8 == s, res[h][off:off + SUBLANES], 0.0)
                    off += SUBLANES
            o[h] = o[h] + _dot(jnp.concatenate(a_parts, 0).astype(BF16), v16[h])
        for h in heads:
            b_last = b[h][C - 1:C, :]
            kd = (k[h] * jnp.exp(b_last - b[h])).astype(BF16)
            st_ref[h] = st[h] * jnp.exp(b_last) + _dot_tn(v16[h], kd)
            y_ref[rs, sls[h]] = (_rms_norm(o[h], g_ref[...]) * gate_ref[rs, sls[h]]).astype(BF16)
        return carry

    lax.fori_loop(0, n_sub, chunk, 0)

    @pl.when(step == pl.num_programs(1) - 1)
    def _():
        for h in range(H_A):
            s_out_ref[h] = st_ref[h].T


def _hgrn(Pa, s0, s0_layer, g, off, B, T, M, y_prev):
    C = min(T, 64)
    rows_blk = min(T, 256)
    nb = T // rows_blk
    base = off // rows_blk
    width = H_A * DV_A

    def rows(colblk):
        return pl.BlockSpec((rows_blk, width), lambda b, c: (base + b * nb + c, colblk))

    s_specs, s_args = ([], []) if s0 is None else ([_state_spec(s0, s0_layer, H_A, DK_A, DV_A)], [s0])
    a_specs, a_args, aliases = _y_alias(y_prev, 5 + len(s_args))
    return pl.pallas_call(
        functools.partial(_hgrn_kernel, C=C, has_s0=bool(s_args)),
        grid=(B, nb),
        in_specs=[rows(0), rows(1), rows(2), rows(3)] + s_specs + [pl.BlockSpec((1, DV_A), lambda b, c: (0, 0))] + a_specs,
        out_specs=(pl.BlockSpec((rows_blk, width), lambda b, c: (base + b * nb + c, 0)),
                   pl.BlockSpec((None, H_A, DK_A, DV_A), lambda b, c: (b, 0, 0, 0))),
        out_shape=(jax.ShapeDtypeStruct((M, N_BRANCH * BRANCH_W), BF16),
                   jax.ShapeDtypeStruct((B, H_A, DK_A, DV_A), F32)),
        scratch_shapes=[pltpu.VMEM((H_A, DV_A, DK_A), F32)],
        input_output_aliases=aliases,
        compiler_params=_cparams(("parallel", "arbitrary")),
        name="hgrn",
    )(Pa, Pa, Pa, Pa, *s_args, g, *a_args)


def _ret_kernel(*refs, has_s0):
    q_ref, k_ref, v_ref, gate_ref = refs[:4]
    s0_ref = refs[4] if has_s0 else None
    g_ref = refs[4 + has_s0]
    y_ref, s_out_ref, s_ref = refs[-3:]
    c = pl.program_id(1)
    C = q_ref.shape[0]

    @pl.when(c == 0)
    def _():
        s_ref[...] = s0_ref[...] if has_s0 else jnp.zeros_like(s_ref)

    ti = lax.broadcasted_iota(jnp.int32, (C, C), 0)
    si = lax.broadcasted_iota(jnp.int32, (C, C), 1)
    dist = (ti - si).astype(F32)
    t1 = (lax.broadcasted_iota(jnp.int32, (C, 1), 0) + 1).astype(F32)

    heads = range(H_B)
    sls = [slice(h * LANES, (h + 1) * LANES) for h in heads]
    log_gamma = [math.log1p(-(2.0 ** (-5.0 - h))) for h in heads]
    q = [q_ref[:, sl] for sl in sls]
    k = [k_ref[:, sl] for sl in sls]
    v16 = [v_ref[:, sl].astype(BF16) for sl in sls]
    s = [s_ref[h] for h in heads]
    a = [_dot_nt(q[h].astype(BF16), k[h].astype(BF16))
         * jnp.exp(jnp.where(ti >= si, dist * log_gamma[h], -jnp.inf)) for h in heads]
    o = [_dot(a[h].astype(BF16), v16[h])
         + _dot((q[h] * jnp.exp(t1 * log_gamma[h])).astype(BF16), s[h].astype(BF16)) for h in heads]
    for h in heads:
        kd = (k[h] * jnp.exp((C - t1) * log_gamma[h])).astype(BF16)
        s_ref[h] = math.exp(C * log_gamma[h]) * s[h] + _dot_tn(kd, v16[h])
    for h in heads:
        mu = jnp.mean(o[h], -1, keepdims=True)
        d = o[h] - mu
        var = jnp.mean(d * d, -1, keepdims=True)
        y = d * lax.rsqrt(var + EPS) * g_ref[...] * gate_ref[:, sls[h]]
        y_ref[:, sls[h]] = y.astype(BF16)

    @pl.when(c == pl.num_programs(1) - 1)
    def _():
        s_out_ref[...] = s_ref[...]


def _ret(Pa, s0, s0_layer, g, off, B, T, M, y_prev):
    C = min(T, 256)
    nc = T // C
    base = off // C
    width = H_B * DV_B

    def rows(colblk):
        return pl.BlockSpec((C, width), lambda b, c: (base + b * nc + c, colblk))

    s_specs, s_args = ([], []) if s0 is None else ([_state_spec(s0, s0_layer, H_B, DK_B, DV_B)], [s0])
    a_specs, a_args, aliases = _y_alias(y_prev, 5 + len(s_args))
    return pl.pallas_call(
        functools.partial(_ret_kernel, has_s0=bool(s_args)),
        grid=(B, nc),
        in_specs=[rows(4), rows(5), rows(6), rows(7)] + s_specs + [pl.BlockSpec((1, DV_B), lambda b, c: (0, 0))] + a_specs,
        out_specs=(pl.BlockSpec((C, width), lambda b, c: (base + b * nc + c, 1)),
                   pl.BlockSpec((None, H_B, DK_B, DV_B), lambda b, c: (b, 0, 0, 0))),
        out_shape=(jax.ShapeDtypeStruct((M, N_BRANCH * BRANCH_W), BF16),
                   jax.ShapeDtypeStruct((B, H_B, DK_B, DV_B), F32)),
        scratch_shapes=[pltpu.VMEM((H_B, DK_B, DV_B), F32)],
        input_output_aliases=aliases,
        compiler_params=_cparams(("parallel", "arbitrary")),
        name="retention",
    )(Pa, Pa, Pa, Pa, *s_args, g, *a_args)


def _past_blocks(n, cap=512):
    out, r = [], 0
    while r < n:
        w = min(cap, n - r)
        out.append((r, w))
        r += w
    return out


def _colmax(mrun, s):
    n = s.shape[1]
    if n % LANES:
        return jnp.maximum(mrun, jnp.max(s, -1, keepdims=True))
    for gi in range(n // LANES):
        mrun = jnp.maximum(mrun, s[:, gi * LANES:(gi + 1) * LANES])
    return mrun


def _attend(n_heads, score_fn, value_fn, n_past, bq, qi, bias, sp_ref, ss_ref, m_ref, acc_ref):
    blocks = _past_blocks(n_past)
    heads = range(n_heads)
    m_ref[...] = jnp.full(m_ref.shape, NEG_BIG, F32)

    def keep(h, dst, idx, s):
        dst[idx] = s
        m_ref[h] = _colmax(m_ref[h], s)

    for r0, n in blocks:
        for h in heads:
            keep(h, sp_ref, (h, slice(None), slice(r0, r0 + n)), score_fn(h, "past", r0, n))

    def scores(j, carry):
        r0 = pl.multiple_of(j * bq, bq)
        for h in heads:
            keep(h, ss_ref, (h, j), score_fn(h, "self", r0, bq))
        return carry

    lax.fori_loop(0, qi, scores, 0)
    r_diag = pl.multiple_of(qi * bq, bq)
    for h in heads:
        s = score_fn(h, "self", r_diag, bq)
        keep(h, ss_ref, (h, qi), s if bias is None else s + bias)
    m = [jnp.max(m_ref[h], -1, keepdims=True) for h in heads]

    acc_ref[...] = jnp.zeros_like(acc_ref)

    def accumulate(h, s, v16):
        p = jnp.exp2(s - m[h]).astype(BF16)
        acc_ref[h] += _dot(p, jnp.concatenate([v16, jnp.ones_like(v16)], 1))

    for r0, n in blocks:
        for h in heads:
            accumulate(h, sp_ref[h, :, r0:r0 + n], value_fn(h, "past", r0, n))

    def weighted(j, carry):
        r0 = pl.multiple_of(j * bq, bq)
        for h in heads:
            accumulate(h, ss_ref[h, j], value_fn(h, "self", r0, bq))
        return carry

    lax.fori_loop(0, qi, weighted, 0)
    for h in heads:
        accumulate(h, ss_ref[h, qi], value_fn(h, "self", r_diag, bq))
    dv = acc_ref.shape[2] // 2
    return [acc_ref[h, :, :dv] / acc_ref[h, :, dv:] for h in heads]


def _chunk_bias(rows, bq):
    if bq <= CHUNK:
        return None
    qc = (lax.broadcasted_iota(jnp.int32, (rows, bq), 0) % bq) // CHUNK
    kc = lax.broadcasted_iota(jnp.int32, (rows, bq), 1) // CHUNK
    return jnp.where(kc <= qc, 0.0, -jnp.inf).astype(F32)


def _attn_scratch(n_heads, rows, n_past, nq, bq, dv):
    return [pltpu.VMEM((n_heads, rows, max(LANES, -(-n_past // LANES) * LANES)), F32),
            pltpu.VMEM((n_heads, nq, rows, bq), F32),
            pltpu.VMEM((n_heads, rows, LANES), F32),
            pltpu.VMEM((n_heads, rows, 2 * dv), F32)]


def _diff_kernel(*refs, past_mode, n_past, lam_init):
    refs = list(refs)
    sp_ref, ss_ref, m_ref, acc_ref = refs[-4:]
    y_ref = refs[-5]
    if past_mode:
        q_ref, ks_ref, vs_ref, kp_ref, vp_ref, lam_ref, g_ref = refs[:7]
    else:
        q_ref, ks_ref, vs_ref, lam_ref, g_ref = refs[:5]
        kp_ref = vp_ref = None
    qi = pl.program_id(1)
    bq = q_ref.shape[0]
    lam_p = lam_ref[...]
    lam = (jnp.exp(jnp.sum(lam_p[0:1] * lam_p[1:2], -1, keepdims=True))
           - jnp.exp(jnp.sum(lam_p[2:3] * lam_p[3:4], -1, keepdims=True)) + lam_init)
    lane = lax.broadcasted_iota(jnp.int32, (bq, LANES), 1)
    bias = _chunk_bias(2 * bq, bq)
    scale = DH_C ** -0.5 * LOG2E
    sls = [slice(h * LANES, (h + 1) * LANES) for h in range(H_C)]
    q2 = []
    for sl in sls:
        qh = q_ref[:, sl]
        zero = jnp.zeros_like(qh)
        q2.append(jnp.concatenate([jnp.where(lane < DH_C, qh, zero), jnp.where(lane >= DH_C, qh, zero)], 0))

    def rows_of(ref, h, src, r0, n):
        if src == "past" and past_mode == "cache":
            return ref[pl.ds(H_C * r0 + h, n, stride=H_C), :]
        return ref[pl.ds(r0, n), sls[h]]

    def score_fn(h, src, r0, n):
        k = rows_of(kp_ref if src == "past" else ks_ref, h, src, r0, n)
        return _dot_nt(q2[h], k.astype(BF16)) * scale

    def value_fn(h, src, r0, n):
        return rows_of(vp_ref if src == "past" else vs_ref, h, src, r0, n).astype(BF16)

    a = _attend(H_C, score_fn, value_fn, n_past, bq, qi, bias, sp_ref, ss_ref, m_ref, acc_ref)
    for h, sl in enumerate(sls):
        d = a[h][:bq] - lam * a[h][bq:]
        y_ref[:, sl] = (_rms_norm(d, g_ref[...]) * (1.0 - lam_init)).astype(BF16)


def _q_block(T):
    for bq in (256, 128):
        if T % bq == 0:
            return bq
    return T


def _diff_attn(P16, past, lam_p, g, off, B, T, M, lam_init, y_prev):
    bq = _q_block(T)
    nq = T // bq
    width = H_C * DV_C
    in_specs = [
        pl.BlockSpec((bq, width), lambda b, i: (off // bq + b * nq + i, 0)),
        pl.BlockSpec((T, width), lambda b, i: (off // T + b, 1)),
        pl.BlockSpec((T, width), lambda b, i: (off // T + b, 2)),
    ]
    args = [P16, P16, P16]
    n_past, past_mode = 0, None
    if past is not None:
        past_mode = past[0]
        if past_mode == "flat":
            _, poff, n_past = past
            in_specs += [pl.BlockSpec((n_past, width), lambda b, i: (poff // n_past + b, 1)),
                         pl.BlockSpec((n_past, width), lambda b, i: (poff // n_past + b, 2))]
            args += [P16, P16]
        else:
            _, pk, pv, layer = past
            n_past = pk.shape[2] // H_C
            spec = pl.BlockSpec((None, None, n_past * H_C, LANES), lambda b, i: (layer, b, 0, 0))
            in_specs += [spec, spec]
            args += [pk, pv]
    in_specs += [pl.BlockSpec((4, DH_C), lambda b, i: (0, 0)), pl.BlockSpec((1, DV_C), lambda b, i: (0, 0))]
    args += [lam_p, g]
    a_specs, a_args, aliases = _y_alias(y_prev, len(args))
    return pl.pallas_call(
        functools.partial(_diff_kernel, past_mode=past_mode, n_past=n_past, lam_init=lam_init),
        grid=(B, nq),
        in_specs=in_specs + a_specs,
        out_specs=pl.BlockSpec((bq, width), lambda b, i: (off // bq + b * nq + i, 2)),
        out_shape=jax.ShapeDtypeStruct((M, N_BRANCH * BRANCH_W), BF16),
        scratch_shapes=_attn_scratch(H_C, 2 * bq, n_past, nq, bq, DV_C),
        input_output_aliases=aliases,
        compiler_params=_cparams(("parallel", "arbitrary")),
        name="diff_attn",
    )(*args, *a_args)


def _mla_kernel(*refs, past_mode, n_past):
    refs = list(refs)
    qi = pl.program_id(1)
    if past_mode == "cache":
        sp_ref, ss_ref, m_ref, acc_ref, kvp_ref = refs[-5:]
        y_ref = refs[-6]
        q_ref, kvs_ref, krs_ref, ckvp_ref, krp_ref, wkv_ref = refs[:6]

        @pl.when(qi == 0)
        def _():
            for r0, n in _past_blocks(n_past):
                kvp_ref[r0:r0 + n, :] = _dot(ckvp_ref[r0:r0 + n, :].astype(BF16), wkv_ref[...]).astype(BF16)
    else:
        sp_ref, ss_ref, m_ref, acc_ref = refs[-4:]
        y_ref = refs[-5]
        if past_mode:
            q_ref, kvs_ref, krs_ref, kvp_ref, krp_ref = refs[:5]
        else:
            q_ref, kvs_ref, krs_ref = refs[:3]
            kvp_ref = krp_ref = None
    bq = q_ref.shape[0]
    bias = _chunk_bias(bq, bq)
    scale = (D_NOPE + D_ROPE) ** -0.5 * LOG2E
    qn = [q_ref[:, h * D_NOPE:(h + 1) * D_NOPE] for h in range(H_D)]
    qr = [q_ref[:, H_D * D_NOPE + h * LANES:H_D * D_NOPE + h * LANES + D_ROPE] for h in range(H_D)]

    def score_fn(h, src, r0, n):
        kv = kvp_ref if src == "past" else kvs_ref
        kr = krp_ref if src == "past" else krs_ref
        ksl = slice(h * (D_NOPE + DV_D), h * (D_NOPE + DV_D) + D_NOPE)
        s = _dot_nt(qn[h], kv[pl.ds(r0, n), ksl]) + _dot_nt(qr[h], kr[pl.ds(r0, n), :D_ROPE].astype(BF16))
        return s * scale

    def value_fn(h, src, r0, n):
        vsl = slice(h * (D_NOPE + DV_D) + D_NOPE, (h + 1) * (D_NOPE + DV_D))
        return (kvp_ref if src == "past" else kvs_ref)[pl.ds(r0, n), vsl]

    a = _attend(H_D, score_fn, value_fn, n_past, bq, qi, bias, sp_ref, ss_ref, m_ref, acc_ref)
    for h in range(H_D):
        y_ref[:, h * DV_D:(h + 1) * DV_D] = a[h].astype(BF16)


def _mla_attn(q16, kv16, P16, past, off, B, T, M, y_prev):
    bq = _q_block(T)
    nq = T // bq
    wq = q16.shape[1]
    wkv = kv16.shape[1]
    kr_blk = ((12 - P16_FIRST) * PROJ_TILE + KV_LORA) // LANES
    in_specs = [
        pl.BlockSpec((bq, wq), lambda b, i: (off // bq + b * nq + i, 0)),
        pl.BlockSpec((T, wkv), lambda b, i: (off // T + b, 0)),
        pl.BlockSpec((T, LANES), lambda b, i: (off // T + b, kr_blk)),
    ]
    args = [q16, kv16, P16]
    n_past, past_mode = 0, None
    scratch = []
    if past is not None:
        past_mode = past[0]
        if past_mode == "flat":
            _, poff, n_past = past
            in_specs += [pl.BlockSpec((n_past, wkv), lambda b, i: (poff // n_past + b, 0)),
                         pl.BlockSpec((n_past, LANES), lambda b, i: (poff // n_past + b, kr_blk))]
            args += [kv16, P16]
        else:
            _, pckv, pkr, w_ukv16, layer = past
            n_past = pckv.shape[2]
            in_specs += [pl.BlockSpec((None, None, n_past, KV_LORA), lambda b, i: (layer, b, 0, 0)),
                         pl.BlockSpec((None, None, n_past, D_ROPE), lambda b, i: (layer, b, 0, 0)),
                         pl.BlockSpec((None, KV_LORA, wkv), lambda b, i: (layer, 0, 0))]
            args += [pckv, pkr, w_ukv16]
            scratch = [pltpu.VMEM((n_past, wkv), BF16)]
    width = H_D * DV_D
    a_specs, a_args, aliases = _y_alias(y_prev, len(args))
    return pl.pallas_call(
        functools.partial(_mla_kernel, past_mode=past_mode, n_past=n_past),
        grid=(B, nq),
        in_specs=in_specs + a_specs,
        out_specs=pl.BlockSpec((bq, width), lambda b, i: (off // bq + b * nq + i, 3)),
        out_shape=jax.ShapeDtypeStruct((M, N_BRANCH * BRANCH_W), BF16),
        scratch_shapes=_attn_scratch(H_D, bq, n_past, nq, bq, DV_D) + scratch,
        input_output_aliases=aliases,
        compiler_params=_cparams(("parallel", "arbitrary")),
        name="mla_attn",
    )(*args, *a_args)


def _rope_table(pos, period, half, rot_dim, theta):
    lane = np.arange(LANES)
    li = lane % period
    first = li < half
    second = (li >= half) & (li < rot_dim)
    idx = np.where(first, li, np.where(second, li - half, 0))
    freq = jnp.power(jnp.float32(theta), -jnp.arange(half, dtype=F32) / half)[idx]
    ang = pos.astype(F32)[:, None] * freq[None, :]
    cos, sin = jnp.cos(ang), jnp.sin(ang)
    rot = jnp.asarray(first | second)[None, :]
    return [jnp.where(rot, cos, 1.0), jnp.where(jnp.asarray(second)[None, :], sin, 0.0),
            jnp.where(jnp.asarray(first)[None, :], -sin, 0.0)]


def _rope_tables(pos):
    tabs = (_rope_table(pos, LANES, DK_B // 2, DK_B, RET_THETA)
            + _rope_table(pos, DH_C, PARTIAL_ROT // 2, PARTIAL_ROT, ROPE_THETA)
            + _rope_table(pos, LANES, D_ROPE // 2, D_ROPE, ROPE_THETA))
    return jnp.concatenate(tabs, axis=1)


def _uq_layout(w_uq):
    depth = w_uq.shape[0]
    w = w_uq.reshape(depth, Q_LORA, H_D, D_NOPE + D_ROPE)
    nope = w[..., :D_NOPE].reshape(depth, Q_LORA, H_D * D_NOPE)
    rope = jnp.pad(w[..., D_NOPE:], ((0, 0), (0, 0), (0, 0), (0, LANES - D_ROPE))).reshape(depth, Q_LORA, H_D * LANES)
    return jnp.concatenate([nope, rope], axis=2).astype(BF16)


def kernel(x_prompt, x_sample, cache_diff_k, cache_diff_v, cache_mla_ckv, cache_mla_krope, state_hgrn, state_ret, meta_tokens, w_ffn1_in, w_ffn1_out, ln1_g, ln1_b, w_in, lb_logits, hgrn_norm_g, ret_norm_g, diff_lambda_q1, diff_lambda_k1, diff_lambda_q2, diff_lambda_k2, diff_norm_g, mla_q_norm_g, mla_kv_norm_g, w_mla_uq, w_mla_ukv, w_branch, w_merge_gate, w_mix_out, ln2_g, ln2_b, w_ffn2_in, w_ffn2_out, ln3_g, ln3_b):
    depth = w_in.shape[0]
    Bp, S, D = x_prompt.shape
    Bs, Ts, _ = x_sample.shape
    n_cache = cache_diff_k.shape[2]
    past = n_cache - N_META
    assert S % (2 * CHUNK) == 0 and Ts <= CHUNK and past % CHUNK == 0 and (past + Ts - 1) // CHUNK == past // CHUNK
    alpha = (2 * depth) ** 0.25

    off_s = Bp * S
    off_m = off_s + Bs * Ts
    M = off_m + Bp * N_META
    assert off_s % Ts == 0 and off_m % N_META == 0
    x = jnp.concatenate([
        x_prompt.reshape(Bp * S, D), x_sample.reshape(Bs * Ts, D),
        jnp.broadcast_to(meta_tokens[None].astype(x_prompt.dtype), (Bp, N_META, D)).reshape(Bp * N_META, D)], 0)
    x = x.astype(F32)
    x16 = x.astype(BF16)
    tab_pos = _rope_tables(jnp.arange(N_META + max(S, past + Ts), dtype=jnp.int32))
    tab = jnp.concatenate([
        jnp.tile(tab_pos[N_META:N_META + S], (Bp, 1)),
        jnp.tile(tab_pos[N_META + past:N_META + past + Ts], (Bs, 1)),
        jnp.tile(tab_pos[:N_META], (Bp, 1))], 0)

    w1i, w1o = w_ffn1_in.astype(BF16), w_ffn1_out.astype(BF16)
    w2i, w2o = w_ffn2_in.astype(BF16), w_ffn2_out.astype(BF16)
    w_in16 = jnp.pad(w_in, ((0, 0), (0, 0), (0, PROJ_COLS - w_in.shape[2]))).astype(BF16)
    w_uq16 = _uq_layout(w_mla_uq)
    w_ukv16 = w_mla_ukv.astype(BF16)
    w_gate16, w_branch16, w_out16 = w_merge_gate.astype(BF16), w_branch.astype(BF16), w_mix_out.astype(BF16)

    la, lc = _lower_bounds(lb_logits)
    groups = (("meta", off_m, Bp, N_META), ("prompt", 0, Bp, S), ("sample", off_s, Bs, Ts))
    row = lambda a: a.reshape(1, -1).astype(F32)
    out_dt = x_prompt.dtype

    ck_all = cache_diff_k.reshape(depth, Bs, n_cache * H_C, 2 * DH_C)
    cv_all = cache_diff_v.reshape(depth, Bs, n_cache * H_C, DV_C)
    sa_all, sb_all = state_hgrn.astype(F32), state_ret.astype(F32)

    def alloc(B, T, tail):
        return jnp.zeros((depth, B, T) + tail, out_dt)

    tails = {"k": (H_C, 2 * DH_C), "v": (H_C, DV_C), "ckv": (KV_LORA,), "kr": (D_ROPE,)}
    p_out = {k: alloc(Bp, N_META + S, t) for k, t in tails.items()}
    s_out = {k: alloc(Bs, Ts, t) for k, t in tails.items()}
    states = {k: [] for k in ("psa", "psb", "ssa", "ssb")}

    for l in range(depth):
        x, x16 = _ffn(x, x16, w1i, w1o, l, row(ln1_g[l]), row(ln1_b[l]), alpha)

        lb2 = jnp.stack([la[l], lc[l]], 0)
        Pa, kc4, vc4, ckr, P16, q_d, kv_new = _proj(x16, w_in16, l, lb2, row(mla_q_norm_g[l]),
                                                    row(mla_kv_norm_g[l]), tab, w_uq16, w_ukv16)
        lam_p = jnp.stack([diff_lambda_q1[l], diff_lambda_k1[l], diff_lambda_q2[l], diff_lambda_k2[l]], 0).astype(F32)
        lam_init = 0.8 - 0.6 * math.exp(-0.3 * l)

        y = None
        sa_meta = sb_meta = None
        for name, off, B, T in groups:
            if name == "meta":
                sa0 = sb0 = None
                s_layer = None
                past_c = past_d = None
            elif name == "prompt":
                sa0, sb0, s_layer = sa_meta, sb_meta, None
                past_c = past_d = ("flat", off_m, N_META)
            else:
                sa0, sb0, s_layer = sa_all, sb_all, l
                past_c = ("cache", ck_all, cv_all, l)
                past_d = ("cache", cache_mla_ckv, cache_mla_krope, w_ukv16, l)
            y, sa = _hgrn(Pa, sa0, s_layer, row(hgrn_norm_g[l]), off, B, T, M, y)
            y, sb = _ret(Pa, sb0, s_layer, row(ret_norm_g[l]), off, B, T, M, y)
            y = _diff_attn(P16, past_c, lam_p, row(diff_norm_g[l]), off, B, T, M, lam_init, y)
            y = _mla_attn(q_d, kv_new, P16, past_d, off, B, T, M, y)
            if name == "meta":
                sa_meta, sb_meta = sa, sb
            elif name == "prompt":
                states["psa"].append(sa); states["psb"].append(sb)
            else:
                states["ssa"].append(sa); states["ssb"].append(sb)

        merged = _merge(x16, y, w_gate16, w_branch16, l)
        x, x16 = _mix_out(x, merged, w_out16, l, row(ln2_g[l]), row(ln2_b[l]), alpha)
        x, x16 = _ffn(x, x16, w2i, w2o, l, row(ln3_g[l]), row(ln3_b[l]), alpha)

        pieces = {"k": kc4.reshape(M, H_C, 2 * DH_C), "v": vc4.reshape(M, H_C, DV_C),
                  "ckv": ckr[:, :KV_LORA], "kr": ckr[:, KV_LORA:KV_LORA + D_ROPE]}
        for key, piece in pieces.items():
            tail = tails[key]
            p_out[key] = p_out[key].at[l, :, :N_META].set(piece[off_m:].reshape((Bp, N_META) + tail))
            p_out[key] = p_out[key].at[l, :, N_META:].set(piece[:off_s].reshape((Bp, S) + tail))
            s_out[key] = s_out[key].at[l].set(piece[off_s:off_m].reshape((Bs, Ts) + tail))

    y_prompt = x[:off_s].reshape(Bp, S, D)
    y_sample = x[off_s:off_m].reshape(Bs, Ts, D)
    st = lambda k: jnp.stack(states[k], 0).astype(out_dt)
    return (y_prompt, y_sample, p_out["k"], p_out["v"], p_out["ckv"], p_out["kr"], st("psa"), st("psb"),
            s_out["k"], s_out["v"], s_out["ckv"], s_out["kr"], st("ssa"), st("ssb"))
```
